```python
import jax
import jax.numpy as jnp
from jax import lax
import numpy as np

D_MODEL = 1024
BATCH = 8
SEQ = 2048
DEPTH = 2
DEC_BATCH = 128
DEC_SEQ = 1
PAST_LEN = 16384
PAGE_SIZE = 128

N_EVEN = (DEPTH + 1) // 2
N_ODD = DEPTH // 2
EPS = 1e-6

A_WIDTH = D_MODEL
A_HEAD_DIM = 64
A_HEADS = A_WIDTH // A_HEAD_DIM
A_GROUPS = 2
A_STATE = 128
A_CONV = 4
A_CONV_DIM = A_WIDTH + 2 * A_GROUPS * A_STATE
SSD_CHUNK = 128

B_WIDTH = D_MODEL
B_CHUNK = 128
B_GROUPS = B_WIDTH // 128

C_WIDTH = D_MODEL
C_CONV = 31

D_WIDTH = D_MODEL
D_HEADS = 16
D_BLOCK = D_WIDTH // D_HEADS
D_CONV = 4
LRU_C = 8.0

EVEN_IN = 2 * A_WIDTH + 2 * A_GROUPS * A_STATE + A_HEADS + 3 * B_WIDTH
EVEN_MIX = A_WIDTH + B_WIDTH
EVEN_SPLITS = [A_WIDTH, A_WIDTH + A_CONV_DIM, A_WIDTH + A_CONV_DIM + A_HEADS,
               A_WIDTH + A_CONV_DIM + A_HEADS + B_WIDTH,
               A_WIDTH + A_CONV_DIM + A_HEADS + 2 * B_WIDTH]
ODD_IN = 3 * C_WIDTH + 2 * D_WIDTH
ODD_MIX = C_WIDTH + D_WIDTH
ODD_SPLITS = [C_WIDTH, 2 * C_WIDTH, 3 * C_WIDTH, 3 * C_WIDTH + D_WIDTH]

kernel_name = 'hybrid_ssd_gmlp_conformer_rglru_step'


def _rmsnorm(x, g):
    xf = x.astype(jnp.float32)
    y = xf * lax.rsqrt(jnp.mean(xf * xf, axis=-1, keepdims=True) + EPS)
    return (y * g.astype(jnp.float32)).astype(x.dtype)


def _layernorm(x, g, b):
    xf = x.astype(jnp.float32)
    mu = jnp.mean(xf, axis=-1, keepdims=True)
    xc = xf - mu
    var = jnp.mean(xc * xc, axis=-1, keepdims=True)
    return (xc * lax.rsqrt(var + EPS) * g.astype(jnp.float32) + b.astype(jnp.float32)).astype(x.dtype)


def _causal_dwconv(x, buf, w, b):
    xp = jnp.concatenate([buf.astype(x.dtype), x], axis=1)
    y = lax.conv_general_dilated(xp, w[:, None, :].astype(x.dtype), window_strides=(1,),
                                 padding='VALID', dimension_numbers=('NWC', 'WIO', 'NWC'),
                                 feature_group_count=x.shape[-1])
    return y + b.astype(x.dtype), xp[:, -(w.shape[0] - 1):]


def _segsum(a):
    cs = jnp.cumsum(a, axis=-1)
    diff = cs[..., :, None] - cs[..., None, :]
    n = a.shape[-1]
    return jnp.where(jnp.tril(jnp.ones((n, n), dtype=bool)), diff, -jnp.inf)


def _ssd_scan(xh, dt, a, bm, cm, h0):
    bsz, L, H, P = xh.shape
    G, N = bm.shape[2], bm.shape[3]
    E = H // G
    T = min(SSD_CHUNK, L)
    pad = (-L) % T
    if pad:
        xh = jnp.pad(xh, ((0, 0), (0, pad), (0, 0), (0, 0)))
        dt = jnp.pad(dt, ((0, 0), (0, pad), (0, 0)))
        bm = jnp.pad(bm, ((0, 0), (0, pad), (0, 0), (0, 0)))
        cm = jnp.pad(cm, ((0, 0), (0, pad), (0, 0), (0, 0)))
    nc = (L + pad) // T
    xs = (xh * dt[..., None]).reshape(bsz, nc, T, G, E, P)
    da = jnp.transpose((dt * a).reshape(bsz, nc, T, G, E), (0, 1, 3, 4, 2))
    bm = bm.reshape(bsz, nc, T, G, N)
    cm = cm.reshape(bsz, nc, T, G, N)
    cs = jnp.cumsum(da, axis=-1)
    lmat = jnp.exp(_segsum(da))
    cb = jnp.einsum('bclgn,bcsgn->bcgls', cm, bm)
    y_diag = jnp.einsum('bcgels,bcsgep->bclgep', cb[:, :, :, None] * lmat, xs)
    decay_states = jnp.exp(cs[..., -1:] - cs)
    states = jnp.einsum('bclgn,bcgel,bclgep->bcgepn', bm, decay_states, xs)
    states = jnp.concatenate([h0.reshape(bsz, 1, G, E, P, N), states], axis=1)
    chunk_tot = jnp.pad(jnp.transpose(cs[..., -1], (0, 2, 3, 1)), ((0, 0), (0, 0), (0, 0), (1, 0)))
    decay_chunk = jnp.exp(_segsum(chunk_tot))
    new_states = jnp.einsum('bgezc,bcgepn->bzgepn', decay_chunk, states)
    y_off = jnp.einsum('bclgn,bcgepn,bcgel->bclgep', cm, new_states[:, :-1], jnp.exp(cs))
    y = (y_diag + y_off).reshape(bsz, nc * T, H, P)[:, :L]
    return y, new_states[:, -1].reshape(bsz, H, P, N)


def _ssd_mixer(z, xbc, dt_raw, conv_buf, h0, conv_w, conv_b, dt_bias, a_log, d_skip, gnorm_g):
    bsz, L, _ = z.shape
    xbc_c, buf_new = _causal_dwconv(xbc, conv_buf, conv_w, conv_b)
    xbc_c = jax.nn.silu(xbc_c.astype(jnp.float32))
    xs, bm, cm = jnp.split(xbc_c, [A_WIDTH, A_WIDTH + A_GROUPS * A_STATE], axis=-1)
    xh = xs.reshape(bsz, L, A_HEADS, A_HEAD_DIM)
    bm = bm.reshape(bsz, L, A_GROUPS, A_STATE)
    cm = cm.reshape(bsz, L, A_GROUPS, A_STATE)
    dt = jax.nn.softplus(dt_raw.astype(jnp.float32) + dt_bias.astype(jnp.float32))
    a = -jnp.exp(a_log.astype(jnp.float32))
    y, h_last = _ssd_scan(xh, dt, a, bm, cm, h0.astype(jnp.float32))
    y = y + xh * d_skip.astype(jnp.float32)[:, None]
    gw = A_WIDTH // A_GROUPS
    y = y.reshape(bsz, L, A_GROUPS, gw) * jax.nn.silu(z.astype(jnp.float32)).reshape(bsz, L, A_GROUPS, gw)
    y = _rmsnorm(y, gnorm_g.reshape(A_GROUPS, gw)).reshape(bsz, L, A_WIDTH)
    return y.astype(z.dtype), h_last.astype(h0.dtype), buf_new


def _chunk_gmlp_mixer(u, v, ln_g, ln_b, w_s, b_s):
    bsz, L, _ = v.shape
    u = jax.nn.gelu(u)
    vn = _layernorm(jax.nn.gelu(v), ln_g, ln_b)
    T = min(B_CHUNK, L)
    pad = (-L) % T
    vp = jnp.pad(vn, ((0, 0), (0, pad), (0, 0)))
    nc = (L + pad) // T
    vg = vp.reshape(bsz, nc, T, B_GROUPS, B_WIDTH // B_GROUPS)
    w = jnp.where(jnp.tril(jnp.ones((T, T), dtype=bool)), w_s[:, :T, :T], 0.0).astype(v.dtype)
    mixed = jnp.einsum('gts,bcsgd->bctgd', w, vg) + b_s[:, :T].T.astype(v.dtype)[None, None, :, :, None]
    mixed = mixed.reshape(bsz, nc * T, B_WIDTH)[:, :L]
    return u * mixed, vn


def _conformer_conv_mixer(ga, gb, conv_buf, conv_w, conv_b, ln_g, ln_b):
    glu = ga * jax.nn.sigmoid(gb)
    c, buf_new = _causal_dwconv(glu, conv_buf, conv_w, conv_b)
    return jax.nn.silu(_layernorm(c, ln_g, ln_b)), buf_new


def _rglru_mixer(xd, conv_buf, h0, conv_w, conv_b, wa, ba, wx, bx, lam):
    xc, buf_new = _causal_dwconv(xd, conv_buf, conv_w, conv_b)
    bsz, L, W = xc.shape
    xf = xc.astype(jnp.float32)
    xb = xf.reshape(bsz, L, D_HEADS, D_BLOCK)
    r = jax.nn.sigmoid(jnp.einsum('blhi,hij->blhj', xb, wa.astype(jnp.float32)).reshape(bsz, L, W) + ba.astype(jnp.float32))
    i = jax.nn.sigmoid(jnp.einsum('blhi,hij->blhj', xb, wx.astype(jnp.float32)).reshape(bsz, L, W) + bx.astype(jnp.float32))
    log_a = -LRU_C * r * jax.nn.softplus(-lam.astype(jnp.float32))
    a = jnp.exp(log_a)
    bterm = jnp.sqrt(-jnp.expm1(2.0 * log_a)) * (i * xf)
    bterm = bterm.at[:, 0].add(a[:, 0] * h0.astype(jnp.float32))

    def combine(e1, e2):
        a1, b1 = e1
        a2, b2 = e2
        return a1 * a2, a2 * b1 + b2

    _, h = lax.associative_scan(combine, (a, bterm), axis=1)
    return h.astype(xd.dtype), h[:, -1].astype(h0.dtype), buf_new


def _trunk(x, ssm0, ssdbuf0, cbuf0, lbuf0, lru0,
           norm_even, w_in_even, ssd_conv_w, ssd_conv_b, ssd_dt_bias, ssd_a_log, ssd_d, ssd_norm,
           gmlp_ln_g, gmlp_ln_b, gmlp_w_s, gmlp_b_s, w_out_even,
           norm_odd, w_in_odd, ccv_w, ccv_b, ccv_ln_g, ccv_ln_b,
           lru_conv_w, lru_conv_b, lru_wa, lru_ba, lru_wx, lru_bx, lru_lambda, w_out_odd, final_norm):
    ssm_n, ssdbuf_n, v_n, cbuf_n, lbuf_n, lru_n = [], [], [], [], [], []
    for layer in range(DEPTH):
        k = layer // 2
        if layer % 2 == 0:
            hn = _rmsnorm(x, norm_even[k])
            proj = jnp.einsum('bld,de->ble', hn, w_in_even[k])
            z, xbc, dt_raw, u, v, g = jnp.split(proj, EVEN_SPLITS, axis=-1)
            ya, ssm_new, sbuf_new = _ssd_mixer(z, xbc, dt_raw, ssdbuf0[k], ssm0[k], ssd_conv_w[k], ssd_conv_b[k],
                                               ssd_dt_bias[k], ssd_a_log[k], ssd_d[k], ssd_norm[k])
            yb, v_rows = _chunk_gmlp_mixer(u, v, gmlp_ln_g[k], gmlp_ln_b[k], gmlp_w_s[k], gmlp_b_s[k])
            mix = jnp.concatenate([ya, yb * jax.nn.silu(g)], axis=-1)
            x = x + jnp.einsum('ble,ed->bld', mix, w_out_even[k])
            ssm_n.append(ssm_new)
            ssdbuf_n.append(sbuf_new)
            v_n.append(v_rows)
        else:
            hn = _rmsnorm(x, norm_odd[k])
            proj = jnp.einsum('bld,de->ble', hn, w_in_odd[k])
            ga, gb, gc, xd, gd = jnp.split(proj, ODD_SPLITS, axis=-1)
            yc, cbuf_new = _conformer_conv_mixer(ga, gb, cbuf0[k], ccv_w[k], ccv_b[k], ccv_ln_g[k], ccv_ln_b[k])
            yd, h_last, lbuf_new = _rglru_mixer(xd, lbuf0[k], lru0[k], lru_conv_w[k], lru_conv_b[k],
                                                lru_wa[k], lru_ba[k], lru_wx[k], lru_bx[k], lru_lambda[k])
            mix = jnp.concatenate([yc * jax.nn.silu(gc), yd * jax.nn.silu(gd)], axis=-1)
            x = x + jnp.einsum('ble,ed->bld', mix, w_out_odd[k])
            cbuf_n.append(cbuf_new)
            lbuf_n.append(lbuf_new)
            lru_n.append(h_last)
    return (_rmsnorm(x, final_norm), jnp.stack(ssm_n), jnp.stack(ssdbuf_n), jnp.stack(v_n),
            jnp.stack(cbuf_n), jnp.stack(lbuf_n), jnp.stack(lru_n))


def setup_inputs(seed: int = 0) -> dict:
    key = jax.random.key(seed)
    ks = iter(jax.random.split(key, 48))

    def nrm(shape, scale=1.0):
        return scale * jax.random.normal(next(ks), shape, jnp.float32)

    def unif(shape, lo, hi):
        return jax.random.uniform(next(ks), shape, jnp.float32, lo, hi)

    dt0 = jnp.exp(unif((N_EVEN, A_HEADS), float(np.log(1e-3)), float(np.log(1e-1))))
    a_c = unif((N_ODD, D_WIDTH), 0.9, 0.999)
    s = a_c ** (1.0 / LRU_C)
    return {
        'x_prompt': nrm((BATCH, SEQ, D_MODEL)),
        'x_sample': nrm((DEC_BATCH, DEC_SEQ, D_MODEL)),
        'state_ssm': nrm((N_EVEN, DEC_BATCH, A_HEADS, A_HEAD_DIM, A_STATE), 0.1),
        'state_ssd_conv': nrm((N_EVEN, DEC_BATCH, A_CONV - 1, A_CONV_DIM)),
        'state_ccv': nrm((N_ODD, DEC_BATCH, C_CONV - 1, C_WIDTH), 0.5),
        'state_lru_conv': nrm((N_ODD, DEC_BATCH, D_CONV - 1, D_WIDTH)),
        'state_lru': nrm((N_ODD, DEC_BATCH, D_WIDTH), 0.5),
        'norm_even': 1.0 + nrm((N_EVEN, D_MODEL), 0.1),
        'w_in_even': nrm((N_EVEN, D_MODEL, EVEN_IN), D_MODEL ** -0.5),
        'ssd_conv_w': nrm((N_EVEN, A_CONV, A_CONV_DIM), A_CONV ** -0.5),
        'ssd_conv_b': nrm((N_EVEN, A_CONV_DIM), 0.02),
        'ssd_dt_bias': dt0 + jnp.log(-jnp.expm1(-dt0)),
        'ssd_a_log': jnp.log(unif((N_EVEN, A_HEADS), 1.0, 16.0)),
        'ssd_d': 1.0 + nrm((N_EVEN, A_HEADS), 0.1),
        'ssd_norm': 1.0 + nrm((N_EVEN, A_WIDTH), 0.1),
        'gmlp_ln_g': 1.0 + nrm((N_EVEN, B_WIDTH), 0.1),
        'gmlp_ln_b': nrm((N_EVEN, B_WIDTH), 0.02),
        'gmlp_w_s': nrm((N_EVEN, B_GROUPS, B_CHUNK, B_CHUNK), B_CHUNK ** -0.5),
        'gmlp_b_s': 1.0 + nrm((N_EVEN, B_GROUPS, B_CHUNK), 0.1),
        'w_out_even': nrm((N_EVEN, EVEN_MIX, D_MODEL), EVEN_MIX ** -0.5),
        'norm_odd': 1.0 + nrm((N_ODD, D_MODEL), 0.1),
        'w_in_odd': nrm((N_ODD, D_MODEL, ODD_IN), D_MODEL ** -0.5),
        'ccv_w': nrm((N_ODD, C_CONV, C_WIDTH), C_CONV ** -0.5),
        'ccv_b': nrm((N_ODD, C_WIDTH), 0.02),
        'ccv_ln_g': 1.0 + nrm((N_ODD, C_WIDTH), 0.1),
        'ccv_ln_b': nrm((N_ODD, C_WIDTH), 0.02),
        'lru_conv_w': nrm((N_ODD, D_CONV, D_WIDTH), D_CONV ** -0.5),
        'lru_conv_b': nrm((N_ODD, D_WIDTH), 0.02),
        'lru_wa': nrm((N_ODD, D_HEADS, D_BLOCK, D_BLOCK), D_BLOCK ** -0.5),
        'lru_ba': nrm((N_ODD, D_WIDTH), 0.02),
        'lru_wx': nrm((N_ODD, D_HEADS, D_BLOCK, D_BLOCK), D_BLOCK ** -0.5),
        'lru_bx': nrm((N_ODD, D_WIDTH), 0.02),
        'lru_lambda': jnp.log(s) - jnp.log1p(-s),
        'w_out_odd': nrm((N_ODD, ODD_MIX, D_MODEL), ODD_MIX ** -0.5),
        'final_norm': 1.0 + nrm((D_MODEL,), 0.1),
    }


def reference(x_prompt, x_sample, state_ssm, state_ssd_conv, state_ccv, state_lru_conv, state_lru,
              norm_even, w_in_even, ssd_conv_w, ssd_conv_b, ssd_dt_bias, ssd_a_log, ssd_d, ssd_norm,
              gmlp_ln_g, gmlp_ln_b, gmlp_w_s, gmlp_b_s, w_out_even,
              norm_odd, w_in_odd, ccv_w, ccv_b, ccv_ln_g, ccv_ln_b,
              lru_conv_w, lru_conv_b, lru_wa, lru_ba, lru_wx, lru_bx, lru_lambda, w_out_odd, final_norm):
    params = (norm_even, w_in_even, ssd_conv_w, ssd_conv_b, ssd_dt_bias, ssd_a_log, ssd_d, ssd_norm,
              gmlp_ln_g, gmlp_ln_b, gmlp_w_s, gmlp_b_s, w_out_even,
              norm_odd, w_in_odd, ccv_w, ccv_b, ccv_ln_g, ccv_ln_b,
              lru_conv_w, lru_conv_b, lru_wa, lru_ba, lru_wx, lru_bx, lru_lambda, w_out_odd, final_norm)
    dt = x_prompt.dtype
    y_prompt, ssm_p, ssdbuf_p, _, cbuf_p, lbuf_p, lru_p = _trunk(
        x_prompt,
        jnp.zeros((N_EVEN, BATCH, A_HEADS, A_HEAD_DIM, A_STATE), dt),
        jnp.zeros((N_EVEN, BATCH, A_CONV - 1, A_CONV_DIM), dt),
        jnp.zeros((N_ODD, BATCH, C_CONV - 1, C_WIDTH), dt),
        jnp.zeros((N_ODD, BATCH, D_CONV - 1, D_WIDTH), dt),
        jnp.zeros((N_ODD, BATCH, D_WIDTH), dt),
        *params)
    y_sample, ssm_s, ssdbuf_s, v_s, cbuf_s, lbuf_s, lru_s = _trunk(
        x_sample, state_ssm, state_ssd_conv, state_ccv, state_lru_conv, state_lru, *params)
    return (y_prompt, y_sample, ssm_p, ssm_s, ssdbuf_p, ssdbuf_s, v_s, cbuf_p, cbuf_s, lbuf_p, lbuf_s, lru_p, lru_s)
```

```python
import functools

import jax
import jax.numpy as jnp
from jax import lax
from jax.experimental import pallas as pl
from jax.experimental.pallas import tpu as pltpu

F32 = jnp.float32
BF16 = jnp.bfloat16

D_MODEL = 1024
EPS = 1e-6
A_HEADS = 16
A_HEAD_DIM = 64
A_STATE = 128
A_GROUPS = 2
A_CONV = 4
A_CONV_DIM = 1536
CHUNK = 128
B_GROUPS = 8
C_CONV = 31
D_CONV = 4
D_HEADS = 16
D_BLOCK = 64
LRU_C = 8.0

P0_Z, P0_U, P0_V, P0_G, P0_DT, P0_XBC = 0, 1024, 2048, 3072, 4096, 4224
P0_MAIN = 4224
P0_ALL = P0_MAIN + A_CONV_DIM

VMEM_LIMIT_BYTES = 56 * 1024 * 1024

T0 = 512
T1 = 64
LANE = 128
SUB = 8
CPAD = 32


def _mm(a, b):
    return jnp.dot(a, b, preferred_element_type=F32)


def _rms(x, g):
    return x * lax.rsqrt(jnp.mean(x * x, axis=-1, keepdims=True) + EPS) * g


def _ln(x, g, b):
    mu = jnp.mean(x, axis=-1, keepdims=True)
    xc = x - mu
    var = jnp.mean(xc * xc, axis=-1, keepdims=True)
    return xc * lax.rsqrt(var + EPS) * g + b


def _split3(v):
    hi = v.astype(BF16)
    r1 = v - hi.astype(F32)
    mid = r1.astype(BF16)
    lo = (r1 - mid.astype(F32)).astype(BF16)
    return hi, mid, lo


def _mm_exact_rhs(m01, v):
    hi, mid, lo = _split3(v)
    return _mm(m01, hi) + _mm(m01, mid) + _mm(m01, lo)


def _mm_exact_lhs(v, m01):
    hi, mid, lo = _split3(v)
    return _mm(hi, m01) + _mm(mid, m01) + _mm(lo, m01)


def _group_rms(y, gn):
    half = y.shape[-1] // A_GROUPS
    parts = []
    for g in range(A_GROUPS):
        yg = y[:, g * half:(g + 1) * half]
        parts.append(yg * lax.rsqrt(jnp.mean(yg * yg, axis=-1, keepdims=True) + EPS))
    return jnp.concatenate(parts, axis=-1) * gn


def _layer0_prompt_kernel(x_ref, ne_ref, win_ref, cw_ref, cb_ref, dtb_ref, alog_ref, dsk_ref, gn_ref,
                          lng_ref, lnb_ref, ws_ref, bsf_ref, wout_ref, ltri_ref, e64_ref, e128_ref,
                          x1_ref, ssm_ref, cbuf_ref,
                          proj_sc, xpad_sc, xc_sc, mix_sc, state_sc):
    c = pl.program_id(1)
    last = pl.num_programs(1) - 1
    T = x_ref.shape[0]

    @pl.when(c == 0)
    def _():
        state_sc[...] = jnp.zeros_like(state_sc)
        xpad_sc[0:SUB, :] = jnp.zeros((SUB, A_CONV_DIM), F32)

    hn = _rms(x_ref[...], ne_ref[...]).astype(BF16)
    for s in range(0, P0_MAIN, 512):
        e = min(s + 512, P0_MAIN)
        proj_sc[:, s:e] = _mm(hn, win_ref[:, s:e])
    for s in range(0, A_CONV_DIM, 512):
        xpad_sc[SUB:SUB + T, s:s + 512] = _mm(hn, win_ref[:, P0_MAIN + s:P0_MAIN + s + 512])

    acc = cb_ref[...] + cw_ref[A_CONV - 1:A_CONV, :] * xpad_sc[SUB:SUB + T, :]
    for k in range(A_CONV - 1):
        off = SUB - (A_CONV - 1) + k
        acc = acc + cw_ref[k:k + 1, :] * xpad_sc[off:off + T, :]
    xc_sc[...] = jax.nn.silu(acc)
    tail = xpad_sc[T:T + SUB, :]

    @pl.when(c == last)
    def _():
        cbuf_ref[...] = tail

    xpad_sc[0:SUB, :] = tail

    tril = lax.broadcasted_iota(jnp.int32, (CHUNK, CHUNK), 0) >= lax.broadcasted_iota(jnp.int32, (CHUNK, CHUNK), 1)
    lane_lo = lax.broadcasted_iota(jnp.int32, (CHUNK, LANE), 1) < A_HEAD_DIM
    a_neg = -jnp.exp(alog_ref[...])
    gw = A_HEADS // A_GROUPS * A_HEAD_DIM

    def sub_chunk(j, carry):
        r0 = pl.multiple_of(j * CHUNK, CHUNK)
        rows = pl.ds(r0, CHUNK)
        xs = xc_sc[rows, 0:1024]
        bm = xc_sc[rows, 1024:1024 + A_GROUPS * A_STATE]
        cm = xc_sc[rows, 1024 + A_GROUPS * A_STATE:A_CONV_DIM]
        dt = jax.nn.softplus(proj_sc[rows, P0_DT:P0_DT + LANE] + dtb_ref[...])
        da = dt * a_neg
        cs = _mm_exact_rhs(ltri_ref[...], da)
        cs_t = cs.T
        dt_e = _mm_exact_lhs(dt, e64_ref[...])
        cs_e = _mm_exact_lhs(cs, e64_ref[...])
        cs_b = _mm_exact_lhs(cs, e128_ref[...])
        xdt = xs * dt_e
        xdt_bf = xdt.astype(BF16)
        xdec_bf = (xdt * jnp.exp(cs_e[CHUNK - 1:CHUNK, :] - cs_e)).astype(BF16)
        grow = jnp.exp(cs_e)

        y_parts = []
        for g in range(A_GROUPS):
            bg = bm[:, g * A_STATE:(g + 1) * A_STATE].astype(BF16)
            cg = cm[:, g * A_STATE:(g + 1) * A_STATE].astype(BF16)
            cb = lax.dot_general(cg, bg, (((1,), (1,)), ((), ())), preferred_element_type=F32)
            st = state_sc[g * gw:(g + 1) * gw, :]
            y_off = lax.dot_general(cg, st.astype(BF16), (((1,), (1,)), ((), ())), preferred_element_type=F32)
            upd = lax.dot_general(xdec_bf[:, g * gw:(g + 1) * gw], bg, (((0,), (0,)), ((), ())),
                                  preferred_element_type=F32)
            for hp in range(A_HEADS // A_GROUPS // 2):
                res = []
                for q in range(2):
                    h = g * (A_HEADS // A_GROUPS) + 2 * hp + q
                    diff = cs_b[:, h * LANE:(h + 1) * LANE] - cs_t[h:h + 1, :]
                    lm = jnp.exp(jnp.where(tril, diff, -jnp.inf))
                    wm = (cb * lm).astype(BF16)
                    res.append(_mm(wm, xdt_bf[:, g * gw + hp * LANE:g * gw + (hp + 1) * LANE]))
                    r_lo = g * gw + (2 * hp + q) * A_HEAD_DIM
                    state_sc[r_lo:r_lo + A_HEAD_DIM, :] = (
                        jnp.exp(cs_t[h:h + 1, CHUNK - 1:CHUNK]) * st[(2 * hp + q) * A_HEAD_DIM:(2 * hp + q + 1) * A_HEAD_DIM, :]
                        + upd[(2 * hp + q) * A_HEAD_DIM:(2 * hp + q + 1) * A_HEAD_DIM, :])
                y_parts.append(jnp.where(lane_lo, res[0], res[1]) + y_off[:, hp * LANE:(hp + 1) * LANE]
                               * grow[:, g * gw + hp * LANE:g * gw + (hp + 1) * LANE])
        y = jnp.concatenate(y_parts, axis=-1) + xs * dsk_ref[...]
        ya = _group_rms(y * jax.nn.silu(proj_sc[rows, P0_Z:P0_Z + 1024]), gn_ref[...])
        mix_sc[rows, 0:1024] = ya.astype(BF16)

        ug = jax.nn.gelu(proj_sc[rows, P0_U:P0_U + 1024])
        vn = _ln(jax.nn.gelu(proj_sc[rows, P0_V:P0_V + 1024]), lng_ref[...], lnb_ref[...])
        vn_bf = vn.astype(BF16)
        mixed = []
        for g in range(B_GROUPS):
            wt = jnp.where(tril, ws_ref[g], 0.0).astype(BF16)
            mixed.append(_mm(wt, vn_bf[:, g * LANE:(g + 1) * LANE]))
        mixed = jnp.concatenate(mixed, axis=-1) + bsf_ref[...]
        yb = ug * mixed * jax.nn.silu(proj_sc[rows, P0_G:P0_G + 1024])
        mix_sc[rows, 1024:2048] = yb.astype(BF16)
        return carry

    lax.fori_loop(0, T // CHUNK, sub_chunk, 0)

    x1_ref[...] = x_ref[...] + _mm(mix_sc[...], wout_ref[...])

    @pl.when(c == last)
    def _():
        ssm_ref[...] = state_sc[...]


def _layer0_prompt(x, p):
    nb, seq, _ = x.shape
    T = T0
    const = lambda shape: pl.BlockSpec(shape, lambda b, c: (0,) * len(shape), pipeline_mode=pl.Buffered(1))
    in_specs = [
        pl.BlockSpec((None, T, D_MODEL), lambda b, c: (b, c, 0)),
        const((1, D_MODEL)), const((D_MODEL, P0_ALL)), const((A_CONV, A_CONV_DIM)), const((1, A_CONV_DIM)),
        const((1, LANE)), const((1, LANE)), const((1, 1024)), const((1, 1024)),
        const((1, 1024)), const((1, 1024)), const((B_GROUPS, CHUNK, CHUNK)), const((CHUNK, 1024)),
        const((2048, D_MODEL)), const((CHUNK, CHUNK)), const((LANE, 1024)), const((LANE, 2048)),
    ]
    out_shape = (jax.ShapeDtypeStruct((nb, seq, D_MODEL), F32),
                 jax.ShapeDtypeStruct((nb, A_HEADS * A_HEAD_DIM, A_STATE), F32),
                 jax.ShapeDtypeStruct((nb, SUB, A_CONV_DIM), F32))
    out_specs = (pl.BlockSpec((None, T, D_MODEL), lambda b, c: (b, c, 0)),
                 pl.BlockSpec((None, A_HEADS * A_HEAD_DIM, A_STATE), lambda b, c: (b, 0, 0)),
                 pl.BlockSpec((None, SUB, A_CONV_DIM), lambda b, c: (b, 0, 0)))
    scratch = [pltpu.VMEM((T, P0_MAIN), F32), pltpu.VMEM((SUB + T, A_CONV_DIM), F32),
               pltpu.VMEM((T, A_CONV_DIM), F32), pltpu.VMEM((T, 2048), BF16),
               pltpu.VMEM((A_HEADS * A_HEAD_DIM, A_STATE), F32)]
    return pl.pallas_call(
        _layer0_prompt_kernel, out_shape=out_shape, grid=(nb, seq // T),
        in_specs=in_specs, out_specs=out_specs, scratch_shapes=scratch,
        compiler_params=pltpu.CompilerParams(dimension_semantics=("arbitrary", "arbitrary"),
                                             vmem_limit_bytes=VMEM_LIMIT_BYTES),
        name="layer0_prompt",
    )(x, p["ne"], p["win0"], p["ssd_cw"], p["ssd_cb"], p["dtb"], p["alog"], p["dsk"], p["gn"],
      p["lng"], p["lnb"], p["ws"], p["bsf"], p["wout0"], p["ltri"], p["e64"], p["e128"])


def _lru_gates(xc, wa_ref, ba_ref, wx_ref, bx_ref, lam_ref):
    xb = xc.astype(BF16)
    nblk = wa_ref.shape[0]
    w = wa_ref.shape[1]
    r = jnp.concatenate([_mm(xb[:, j * w:(j + 1) * w], wa_ref[j]) for j in range(nblk)], axis=-1)
    i = jnp.concatenate([_mm(xb[:, j * w:(j + 1) * w], wx_ref[j]) for j in range(nblk)], axis=-1)
    r = jax.nn.sigmoid(r + ba_ref[...])
    i = jax.nn.sigmoid(i + bx_ref[...])
    log_a = -LRU_C * r * jax.nn.softplus(-lam_ref[...])
    a = jnp.exp(log_a)
    bt = jnp.sqrt(1.0 - a * a) * (i * xc)
    return a, bt


def _layer1_prompt_kernel(x_ref, no_ref, win_ref, ccw_ref, ccb_ref, clg_ref, clb_ref, lcw_ref, lcb_ref,
                          wa_ref, ba_ref, wx_ref, bx_ref, lam_ref, wout_ref, fn_ref,
                          y_ref, ccv_ref, lcv_ref, lru_ref,
                          cpad_sc, lpad_sc, sgc_sc, sgd_sc, xc_sc, a_sc, b_sc, h_sc, mix_sc):
    c = pl.program_id(0)
    last = pl.num_programs(0) - 1
    nb, T, _ = x_ref.shape
    rows_all = nb * T
    ncol = D_MODEL // LANE

    @pl.when(c == 0)
    def _():
        cpad_sc[:, 0:CPAD, :] = jnp.zeros((nb, CPAD, D_MODEL), F32)
        lpad_sc[:, 0:SUB, :] = jnp.zeros((nb, SUB, D_MODEL), F32)
        h_sc[...] = jnp.zeros_like(h_sc)

    x = x_ref[...].reshape(rows_all, D_MODEL)
    hn = _rms(x, no_ref[...]).astype(BF16)
    glu = _mm(hn, win_ref[:, 0:1024]) * jax.nn.sigmoid(_mm(hn, win_ref[:, 1024:2048]))
    xd = _mm(hn, win_ref[:, 3072:4096])
    for b in range(nb):
        cpad_sc[b, CPAD:CPAD + T, :] = glu[b * T:(b + 1) * T, :]
        lpad_sc[b, SUB:SUB + T, :] = xd[b * T:(b + 1) * T, :]
    sgc_sc[...] = jax.nn.silu(_mm(hn, win_ref[:, 2048:3072]))
    sgd_sc[...] = jax.nn.silu(_mm(hn, win_ref[:, 4096:5120]))

    def per_seq(b, carry):
        rows = pl.ds(pl.multiple_of(b * T, T), T)
        acc = ccb_ref[...] + ccw_ref[0:1, :] * cpad_sc[b, CPAD - (C_CONV - 1):CPAD - (C_CONV - 1) + T, :]
        for k in range(1, C_CONV):
            off = CPAD - (C_CONV - 1) + k
            acc = acc + ccw_ref[k:k + 1, :] * cpad_sc[b, off:off + T, :]
        yc = jax.nn.silu(_ln(acc, clg_ref[...], clb_ref[...])) * sgc_sc[rows, :]
        mix_sc[rows, 0:1024] = yc.astype(BF16)
        acc = lcb_ref[...] + lcw_ref[0:1, :] * lpad_sc[b, SUB - (D_CONV - 1):SUB - (D_CONV - 1) + T, :]
        for k in range(1, D_CONV):
            off = SUB - (D_CONV - 1) + k
            acc = acc + lcw_ref[k:k + 1, :] * lpad_sc[b, off:off + T, :]
        xc_sc[rows, :] = acc
        return carry

    lax.fori_loop(0, nb, per_seq, 0)

    a, bt = _lru_gates(xc_sc[...], wa_ref, ba_ref, wx_ref, bx_ref, lam_ref)
    for j in range(ncol):
        a_sc[j] = a[:, j * LANE:(j + 1) * LANE]
        b_sc[j] = bt[:, j * LANE:(j + 1) * LANE]

    def step(t, hs):
        out = []
        for j in range(ncol):
            h = a_sc[j, pl.ds(t, nb, stride=T), :] * hs[j] + b_sc[j, pl.ds(t, nb, stride=T), :]
            b_sc[j, pl.ds(t, nb, stride=T), :] = h
            out.append(h)
        return tuple(out)

    hs = lax.fori_loop(0, T, step, tuple(h_sc[j] for j in range(ncol)), unroll=8)
    for j in range(ncol):
        h_sc[j] = hs[j]
        mix_sc[:, 1024 + j * LANE:1024 + (j + 1) * LANE] = (b_sc[j] * sgd_sc[:, j * LANE:(j + 1) * LANE]).astype(BF16)

    out = x + _mm(mix_sc[...], wout_ref[...])
    y_ref[...] = _rms(out, fn_ref[...]).reshape(nb, T, D_MODEL)

    ctail = cpad_sc[:, T:T + CPAD, :]
    ltail = lpad_sc[:, T:T + SUB, :]

    @pl.when(c == last)
    def _():
        ccv_ref[...] = ctail
        lcv_ref[...] = ltail
        for j in range(ncol):
            lru_ref[:, j * LANE:(j + 1) * LANE] = hs[j]

    cpad_sc[:, 0:CPAD, :] = ctail
    lpad_sc[:, 0:SUB, :] = ltail


def _layer1_prompt(x1, p):
    nb, seq, _ = x1.shape
    T = T1
    rows_all = nb * T
    const = lambda shape: pl.BlockSpec(shape, lambda c: (0,) * len(shape), pipeline_mode=pl.Buffered(1))
    nblk, wblk = p["wa"].shape[0], p["wa"].shape[1]
    in_specs = [
        pl.BlockSpec((nb, T, D_MODEL), lambda c: (0, c, 0)),
        const((1, D_MODEL)), const((D_MODEL, 5120)), const((C_CONV, 1024)), const((1, 1024)), const((1, 1024)),
        const((1, 1024)), const((D_CONV, 1024)), const((1, 1024)),
        const((nblk, wblk, wblk)), const((1, 1024)), const((nblk, wblk, wblk)), const((1, 1024)), const((1, 1024)),
        const((2048, D_MODEL)), const((1, D_MODEL)),
    ]
    out_shape = (jax.ShapeDtypeStruct((nb, seq, D_MODEL), F32),
                 jax.ShapeDtypeStruct((nb, CPAD, 1024), F32),
                 jax.ShapeDtypeStruct((nb, SUB, 1024), F32),
                 jax.ShapeDtypeStruct((nb, 1024), F32))
    out_specs = (pl.BlockSpec((nb, T, D_MODEL), lambda c: (0, c, 0)),
                 pl.BlockSpec((nb, CPAD, 1024), lambda c: (0, 0, 0)),
                 pl.BlockSpec((nb, SUB, 1024), lambda c: (0, 0, 0)),
                 pl.BlockSpec((nb, 1024), lambda c: (0, 0)))
    ncol = D_MODEL // LANE
    scratch = [pltpu.VMEM((nb, CPAD + T, 1024), F32), pltpu.VMEM((nb, SUB + T, 1024), F32),
               pltpu.VMEM((rows_all, 1024), F32), pltpu.VMEM((rows_all, 1024), F32), pltpu.VMEM((rows_all, 1024), F32),
               pltpu.VMEM((ncol, rows_all, LANE), F32), pltpu.VMEM((ncol, rows_all, LANE), F32),
               pltpu.VMEM((ncol, nb, LANE), F32), pltpu.VMEM((rows_all, 2048), BF16)]
    return pl.pallas_call(
        _layer1_prompt_kernel, out_shape=out_shape, grid=(seq // T,),
        in_specs=in_specs, out_specs=out_specs, scratch_shapes=scratch,
        compiler_params=pltpu.CompilerParams(dimension_semantics=("arbitrary",),
                                             vmem_limit_bytes=VMEM_LIMIT_BYTES),
        name="layer1_prompt",
    )(x1, p["no"], p["win1"], p["ccw"], p["ccb"], p["clg"], p["clb"], p["lcw"], p["lcb"],
      p["wa"], p["ba"], p["wx"], p["bx"], p["lam"], p["wout1"], p["fn"])


def _sample0_proj_kernel(x_ref, ne_ref, win_ref, sbuf_ref, cw_ref, cb_ref, dtb_ref, alog_ref, dsk_ref,
                         lng_ref, lnb_ref, w00_ref, b00_ref, e64_ref,
                         nbuf_ref, xdt_ref, dec_ref, bc_ref, xsd_ref, sz_ref, ybg_ref, vn_ref):
    hn = _rms(x_ref[...], ne_ref[...]).astype(BF16)
    xbc = _mm(hn, win_ref[:, P0_MAIN:P0_ALL])
    acc = cb_ref[...] + cw_ref[A_CONV - 1:A_CONV, :] * xbc
    for k in range(A_CONV - 1):
        acc = acc + cw_ref[k:k + 1, :] * sbuf_ref[:, k * A_CONV_DIM:(k + 1) * A_CONV_DIM]
    nbuf_ref[:, 0:(A_CONV - 2) * A_CONV_DIM] = sbuf_ref[:, A_CONV_DIM:(A_CONV - 1) * A_CONV_DIM]
    nbuf_ref[:, (A_CONV - 2) * A_CONV_DIM:(A_CONV - 1) * A_CONV_DIM] = xbc
    xcv = jax.nn.silu(acc)
    xs = xcv[:, 0:1024]
    bc_ref[...] = xcv[:, 1024:A_CONV_DIM]
    dt = jax.nn.softplus(_mm(hn, win_ref[:, P0_DT:P0_DT + LANE]) + dtb_ref[...])
    dec = jnp.exp(dt * (-jnp.exp(alog_ref[...])))
    xdt_ref[...] = xs * _mm_exact_lhs(dt, e64_ref[...])
    dec_ref[...] = _mm_exact_lhs(dec, e64_ref[...])
    xsd_ref[...] = xs * dsk_ref[...]
    sz_ref[...] = jax.nn.silu(_mm(hn, win_ref[:, P0_Z:P0_Z + 1024]))
    vn = _ln(jax.nn.gelu(_mm(hn, win_ref[:, P0_V:P0_V + 1024])), lng_ref[...], lnb_ref[...])
    vn_ref[...] = vn
    mixed = w00_ref[...] * vn + b00_ref[...]
    ybg_ref[...] = (jax.nn.gelu(_mm(hn, win_ref[:, P0_U:P0_U + 1024])) * mixed
                    * jax.nn.silu(_mm(hn, win_ref[:, P0_G:P0_G + 1024])))


def _sample0_proj(xs, sbuf, p):
    n = xs.shape[0]
    full = lambda a: pl.BlockSpec(a.shape, lambda i: (0,) * a.ndim)
    args = (xs, p["ne"], p["win0"], sbuf, p["ssd_cw"], p["ssd_cb"], p["dtb"], p["alog"], p["dsk"],
            p["lng"], p["lnb"], p["w00"], p["b00"], p["e64"])
    shapes = [(n, sbuf.shape[1]), (n, 1024), (n, 1024), (n, 2 * A_GROUPS * A_STATE), (n, 1024), (n, 1024),
              (n, 1024), (n, 1024)]
    return pl.pallas_call(
        _sample0_proj_kernel, grid=(1,),
        out_shape=tuple(jax.ShapeDtypeStruct(s, F32) for s in shapes),
        in_specs=[full(a) for a in args],
        out_specs=tuple(pl.BlockSpec(s, lambda i: (0, 0)) for s in shapes),
        compiler_params=pltpu.CompilerParams(dimension_semantics=("arbitrary",), vmem_limit_bytes=VMEM_LIMIT_BYTES),
        name="sample0_proj",
    )(*args)


def _sample0_state_kernel(h_ref, xdt_ref, dec_ref, bc_ref, hn_ref, y_ref):
    tb = h_ref.shape[0]
    eye = (lax.broadcasted_iota(jnp.int32, (A_HEAD_DIM, A_HEAD_DIM), 0)
           == lax.broadcasted_iota(jnp.int32, (A_HEAD_DIM, A_HEAD_DIM), 1))
    hpg = A_HEADS // A_GROUPS
    for b in range(tb):
        for h in range(A_HEADS):
            g = h // hpg
            lo = h * A_HEAD_DIM
            brow = bc_ref[b:b + 1, g * A_STATE:(g + 1) * A_STATE]
            crow = bc_ref[b:b + 1, (A_GROUPS + g) * A_STATE:(A_GROUPS + g + 1) * A_STATE]
            xcol = jnp.sum(jnp.where(eye, xdt_ref[b:b + 1, lo:lo + A_HEAD_DIM], 0.0), axis=1, keepdims=True)
            hnew = dec_ref[b:b + 1, lo:lo + 1] * h_ref[b, h] + xcol * brow
            hn_ref[b, h] = hnew
            ycol = jnp.sum(hnew * crow, axis=1, keepdims=True)
            y_ref[b:b + 1, lo:lo + A_HEAD_DIM] = jnp.sum(jnp.where(eye, ycol, 0.0), axis=0, keepdims=True)


def _sample0_state(h0, xdt, dec, bc):
    n = h0.shape[0]
    tb = SUB
    blk4 = pl.BlockSpec((tb, A_HEADS, A_HEAD_DIM, A_STATE), lambda i: (i, 0, 0, 0))
    row = lambda w: pl.BlockSpec((tb, w), lambda i: (i, 0))
    return pl.pallas_call(
        _sample0_state_kernel, grid=(n // tb,),
        out_shape=(jax.ShapeDtypeStruct(h0.shape, F32), jax.ShapeDtypeStruct((n, 1024), F32)),
        in_specs=[blk4, row(1024), row(1024), row(bc.shape[1])],
        out_specs=(blk4, row(1024)),
        compiler_params=pltpu.CompilerParams(dimension_semantics=("arbitrary",), vmem_limit_bytes=VMEM_LIMIT_BYTES),
        name="sample0_state",
    )(h0, xdt, dec, bc)


def _sample_mid_kernel(y_ref, xsd_ref, sz_ref, ybg_ref, x_ref, gn_ref, wout0_ref,
                       no_ref, win_ref, lbuf_ref, h0_ref, lcw_ref, lcb_ref,
                       wa_ref, ba_ref, wx_ref, bx_ref, lam_ref,
                       x1_ref, glu_ref, sgc_ref, ydg_ref, nlbuf_ref, hnew_ref):
    ya = _group_rms((y_ref[...] + xsd_ref[...]) * sz_ref[...], gn_ref[...])
    mix = jnp.concatenate([ya, ybg_ref[...]], axis=-1).astype(BF16)
    x1 = x_ref[...] + _mm(mix, wout0_ref[...])
    x1_ref[...] = x1
    hn = _rms(x1, no_ref[...]).astype(BF16)
    glu_ref[...] = _mm(hn, win_ref[:, 0:1024]) * jax.nn.sigmoid(_mm(hn, win_ref[:, 1024:2048]))
    sgc_ref[...] = jax.nn.silu(_mm(hn, win_ref[:, 2048:3072]))
    xd = _mm(hn, win_ref[:, 3072:4096])
    acc = lcb_ref[...] + lcw_ref[D_CONV - 1:D_CONV, :] * xd
    for k in range(D_CONV - 1):
        acc = acc + lcw_ref[k:k + 1, :] * lbuf_ref[:, k * 1024:(k + 1) * 1024]
    nlbuf_ref[:, 0:(D_CONV - 2) * 1024] = lbuf_ref[:, 1024:(D_CONV - 1) * 1024]
    nlbuf_ref[:, (D_CONV - 2) * 1024:(D_CONV - 1) * 1024] = xd
    a, bt = _lru_gates(acc, wa_ref, ba_ref, wx_ref, bx_ref, lam_ref)
    h = a * h0_ref[...] + bt
    hnew_ref[...] = h
    ydg_ref[...] = h * jax.nn.silu(_mm(hn, win_ref[:, 4096:5120]))


def _sample_mid(y, xsd, sz, ybg, xs, lbuf, h0, p):
    n = xs.shape[0]
    full = lambda a: pl.BlockSpec(a.shape, lambda i: (0,) * a.ndim)
    args = (y, xsd, sz, ybg, xs, p["gn"], p["wout0"], p["no"], p["win1"], lbuf, h0, p["lcw"], p["lcb"],
            p["wa"], p["ba"], p["wx"], p["bx"], p["lam"])
    shapes = [(n, 1024), (n, 1024), (n, 1024), (n, 1024), (n, lbuf.shape[1]), (n, 1024)]
    return pl.pallas_call(
        _sample_mid_kernel, grid=(1,),
        out_shape=tuple(jax.ShapeDtypeStruct(s, F32) for s in shapes),
        in_specs=[full(a) for a in args],
        out_specs=tuple(pl.BlockSpec(s, lambda i: (0, 0)) for s in shapes),
        compiler_params=pltpu.CompilerParams(dimension_semantics=("arbitrary",), vmem_limit_bytes=VMEM_LIMIT_BYTES),
        name="sample_mid",
    )(*args)


def _sample1_tail_kernel(cbuf_ref, glu_ref, sgc_ref, ydg_ref, x1_ref, ccw_ref, ccb_ref, clg_ref, clb_ref,
                         wout_ref, fn_ref, ncbuf_ref, y_ref):
    glu = glu_ref[...]
    acc = ccb_ref[...] + ccw_ref[C_CONV - 1:C_CONV, :] * glu
    for k in range(C_CONV - 1):
        acc = acc + ccw_ref[k:k + 1, :] * cbuf_ref[:, k * 1024:(k + 1) * 1024]
    ncbuf_ref[:, 0:(C_CONV - 2) * 1024] = cbuf_ref[:, 1024:(C_CONV - 1) * 1024]
    ncbuf_ref[:, (C_CONV - 2) * 1024:(C_CONV - 1) * 1024] = glu
    yc = jax.nn.silu(_ln(acc, clg_ref[...], clb_ref[...])) * sgc_ref[...]
    mix = jnp.concatenate([yc, ydg_ref[...]], axis=-1).astype(BF16)
    out = x1_ref[...] + _mm(mix, wout_ref[...])
    y_ref[...] = _rms(out, fn_ref[...])


def _sample1_tail(cbuf, glu, sgc, ydg, x1, p):
    n = x1.shape[0]
    tb = 16
    row = lambda w: pl.BlockSpec((tb, w), lambda i: (i, 0))
    const = lambda a: pl.BlockSpec(a.shape, lambda i: (0,) * a.ndim, pipeline_mode=pl.Buffered(1))
    consts = (p["ccw"], p["ccb"], p["clg"], p["clb"], p["wout1"], p["fn"])
    return pl.pallas_call(
        _sample1_tail_kernel, grid=(n // tb,),
        out_shape=(jax.ShapeDtypeStruct(cbuf.shape, F32), jax.ShapeDtypeStruct((n, 1024), F32)),
        in_specs=[row(cbuf.shape[1]), row(1024), row(1024), row(1024), row(1024)] + [const(a) for a in consts],
        out_specs=(row(cbuf.shape[1]), row(1024)),
        compiler_params=pltpu.CompilerParams(dimension_semantics=("arbitrary",), vmem_limit_bytes=VMEM_LIMIT_BYTES),
        name="sample1_tail",
    )(cbuf, glu, sgc, ydg, x1, *consts)


def _block_diag(w, per_block):
    nh, d, _ = w.shape
    nblk = nh // per_block
    w = w.reshape(nblk, per_block, d, d)
    eye = jnp.eye(per_block, dtype=w.dtype)
    bd = jnp.einsum("nhij,hk->nhikj", w, eye)
    return bd.reshape(nblk, per_block * d, per_block * d).astype(BF16)


def _prepare(norm_even, w_in_even, ssd_conv_w, ssd_conv_b, ssd_dt_bias, ssd_a_log, ssd_d, ssd_norm,
             gmlp_ln_g, gmlp_ln_b, gmlp_w_s, gmlp_b_s, w_out_even,
             norm_odd, w_in_odd, ccv_w, ccv_b, ccv_ln_g, ccv_ln_b,
             lru_conv_w, lru_conv_b, lru_wa, lru_ba, lru_wx, lru_bx, lru_lambda, w_out_odd, final_norm):
    row = lambda v: v.reshape(1, -1).astype(F32)
    w0 = w_in_even[0]
    z, xbc, dtw, u, v, g = jnp.split(w0, [1024, 2560, 2576, 3600, 4624], axis=-1)
    dtw = jnp.pad(dtw, ((0, 0), (0, LANE - A_HEADS)))
    heads = jnp.arange(LANE)[:, None]
    p = {
        "ne": row(norm_even[0]),
        "win0": jnp.concatenate([z, u, v, g, dtw, xbc], axis=-1).astype(BF16),
        "ssd_cw": ssd_conv_w[0], "ssd_cb": row(ssd_conv_b[0]),
        "dtb": jnp.pad(row(ssd_dt_bias[0]), ((0, 0), (0, LANE - A_HEADS))),
        "alog": jnp.pad(row(ssd_a_log[0]), ((0, 0), (0, LANE - A_HEADS))),
        "dsk": row(jnp.repeat(ssd_d[0], A_HEAD_DIM)),
        "gn": row(ssd_norm[0]),
        "lng": row(gmlp_ln_g[0]), "lnb": row(gmlp_ln_b[0]),
        "ws": gmlp_w_s[0],
        "bsf": jnp.repeat(gmlp_b_s[0].T, LANE, axis=1),
        "w00": row(jnp.repeat(gmlp_w_s[0, :, 0, 0], LANE)), "b00": row(jnp.repeat(gmlp_b_s[0, :, 0], LANE)),
        "wout0": w_out_even[0].astype(BF16),
        "ltri": (jnp.arange(CHUNK)[:, None] >= jnp.arange(CHUNK)[None, :]).astype(BF16),
        "e64": (jnp.arange(1024)[None, :] // A_HEAD_DIM == heads).astype(BF16),
        "e128": (jnp.arange(2048)[None, :] // LANE == heads).astype(BF16),
        "no": row(norm_odd[0]),
        "win1": w_in_odd[0].astype(BF16),
        "ccw": ccv_w[0], "ccb": row(ccv_b[0]), "clg": row(ccv_ln_g[0]), "clb": row(ccv_ln_b[0]),
        "lcw": lru_conv_w[0], "lcb": row(lru_conv_b[0]),
        "wa": _block_diag(lru_wa[0], 4), "ba": row(lru_ba[0]),
        "wx": _block_diag(lru_wx[0], 4), "bx": row(lru_bx[0]),
        "lam": row(lru_lambda[0]),
        "wout1": w_out_odd[0].astype(BF16),
        "fn": row(final_norm),
    }
    return p


def kernel(x_prompt, x_sample, state_ssm, state_ssd_conv, state_ccv, state_lru_conv, state_lru, norm_even, w_in_even, ssd_conv_w, ssd_conv_b, ssd_dt_bias, ssd_a_log, ssd_d, ssd_norm, gmlp_ln_g, gmlp_ln_b, gmlp_w_s, gmlp_b_s, w_out_even, norm_odd, w_in_odd, ccv_w, ccv_b, ccv_ln_g, ccv_ln_b, lru_conv_w, lru_conv_b, lru_wa, lru_ba, lru_wx, lru_bx, lru_lambda, w_out_odd, final_norm):
    p = _prepare(norm_even, w_in_even, ssd_conv_w, ssd_conv_b, ssd_dt_bias, ssd_a_log, ssd_d, ssd_norm,
                 gmlp_ln_g, gmlp_ln_b, gmlp_w_s, gmlp_b_s, w_out_even,
                 norm_odd, w_in_odd, ccv_w, ccv_b, ccv_ln_g, ccv_ln_b,
                 lru_conv_w, lru_conv_b, lru_wa, lru_ba, lru_wx, lru_bx, lru_lambda, w_out_odd, final_norm)
    nb = x_prompt.shape[0]
    ns = x_sample.shape[0]

    x1, ssm_p, sbuf_p = _layer0_prompt(x_prompt, p)
    y_prompt, cbuf_p, lbuf_p, lru_p = _layer1_prompt(x1, p)

    xs = x_sample.reshape(ns, D_MODEL)
    nsbuf, xdt, dec, bc, xsd, sz, ybg, vn = _sample0_proj(xs, state_ssd_conv[0].reshape(ns, -1), p)
    ssm_s, y_ssd = _sample0_state(state_ssm[0], xdt, dec, bc)
    x1s, glu, sgc, ydg, nlbuf, lru_s = _sample_mid(y_ssd, xsd, sz, ybg, xs, state_lru_conv[0].reshape(ns, -1),
                                                   state_lru[0], p)
    ncbuf, y_sample = _sample1_tail(state_ccv[0].reshape(ns, -1), glu, sgc, ydg, x1s, p)

    return (y_prompt,
            y_sample.reshape(ns, 1, D_MODEL),
            ssm_p.reshape(1, nb, A_HEADS, A_HEAD_DIM, A_STATE),
            ssm_s[None],
            sbuf_p[:, SUB - (A_CONV - 1):, :][None],
            nsbuf.reshape(1, ns, A_CONV - 1, A_CONV_DIM),
            vn.reshape(1, ns, 1, 1024),
            cbuf_p[:, CPAD - (C_CONV - 1):, :][None],
            ncbuf.reshape(1, ns, C_CONV - 1, 1024),
            lbuf_p[:, SUB - (D_CONV - 1):, :][None],
            nlbuf.reshape(1, ns, D_CONV - 1, 1024),
            lru_p[None],
            lru_s[None])
```

```python
import functools

import jax
import jax.numpy as jnp
from jax import lax
from jax.experimental import pallas as pl
from jax.experimental.pallas import tpu as pltpu

F32 = jnp.float32
BF16 = jnp.bfloat16

D_MODEL = 1024
EPS = 1e-6
A_HEADS = 16
A_HEAD_DIM = 64
A_STATE = 128
A_GROUPS = 2
A_CONV = 4
A_CONV_DIM = 1536
CHUNK = 128
B_GROUPS = 8
C_CONV = 31
D_CONV = 4
D_HEADS = 16
D_BLOCK = 64
LRU_C = 8.0

P0_Z, P0_U, P0_V, P0_G, P0_DT, P0_XBC = 0, 1024, 2048, 3072, 4096, 4224
P0_MAIN = 4224
P0_ALL = P0_MAIN + A_CONV_DIM

VMEM_LIMIT_BYTES = 56 * 1024 * 1024

T0 = 512
T1 = 64
LANE = 128
SUB = 8
CPAD = 32


def _mm(a, b):
    return jnp.dot(a, b, preferred_element_type=F32)


def _rms(x, g):
    return x * lax.rsqrt(jnp.mean(x * x, axis=-1, keepdims=True) + EPS) * g


def _ln(x, g, b):
    mu = jnp.mean(x, axis=-1, keepdims=True)
    xc = x - mu
    var = jnp.mean(xc * xc, axis=-1, keepdims=True)
    return xc * lax.rsqrt(var + EPS) * g + b


def _split3(v):
    hi = v.astype(BF16)
    r1 = v - hi.astype(F32)
    mid = r1.astype(BF16)
    lo = (r1 - mid.astype(F32)).astype(BF16)
    return hi, mid, lo


def _mm_exact_rhs(m01, v):
    hi, mid, lo = _split3(v)
    return _mm(m01, hi) + _mm(m01, mid) + _mm(m01, lo)


def _mm_exact_lhs(v, m01):
    hi, mid, lo = _split3(v)
    return _mm(hi, m01) + _mm(mid, m01) + _mm(lo, m01)


def _group_rms(y, gn):
    half = y.shape[-1] // A_GROUPS
    parts = []
    for g in range(A_GROUPS):
        yg = y[:, g * half:(g + 1) * half]
        parts.append(yg * lax.rsqrt(jnp.mean(yg * yg, axis=-1, keepdims=True) + EPS))
    return jnp.concatenate(parts, axis=-1) * gn


def _dwconv_block(xf, w_ref, lanes, n_taps, first_off, t_rows):
    n = xf.shape[0]
    acc = None
    for r in range(SUB):
        taps = [(off // SUB, off - first_off) for off in range(first_off, first_off + n_taps) if off % SUB == r]
        if not taps:
            continue
        xr = xf if r == 0 else pltpu.roll(xf, n - r, axis=0)
        for q, k in taps:
            term = w_ref[k:k + 1, lanes] * xr[SUB * q:SUB * q + t_rows]
            acc = term if acc is None else acc + term
    return acc


def _layer0_prompt_kernel(x_ref, ne_ref, win_ref, cw_ref, cb_ref, dtb_ref, alog_ref, dsk_ref, gn_ref,
                          lng_ref, lnb_ref, ws_ref, bsf_ref, wout_ref, ltri_ref, e64_ref, e128_ref,
                          x1_ref, ssm_ref, cbuf_ref,
                          proj_sc, xpad_sc, xc_sc, mix_sc, state_sc):
    c = pl.program_id(1)
    last = pl.num_programs(1) - 1
    T = x_ref.shape[0]

    @pl.when(c == 0)
    def _():
        state_sc[...] = jnp.zeros_like(state_sc)
        xpad_sc[0:SUB, :] = jnp.zeros((SUB, A_CONV_DIM), F32)

    hn = _rms(x_ref[...], ne_ref[...]).astype(BF16)
    for s in range(0, P0_MAIN, 512):
        e = min(s + 512, P0_MAIN)
        proj_sc[:, s:e] = _mm(hn, win_ref[:, s:e])
    for s in range(0, A_CONV_DIM, 512):
        xpad_sc[SUB:SUB + T, s:s + 512] = _mm(hn, win_ref[:, P0_MAIN + s:P0_MAIN + s + 512])

    acc = cb_ref[...] + cw_ref[A_CONV - 1:A_CONV, :] * xpad_sc[SUB:SUB + T, :]
    for k in range(A_CONV - 1):
        off = SUB - (A_CONV - 1) + k
        acc = acc + cw_ref[k:k + 1, :] * xpad_sc[off:off + T, :]
    xc_sc[...] = jax.nn.silu(acc)
    tail = xpad_sc[T:T + SUB, :]

    @pl.when(c == last)
    def _():
        cbuf_ref[...] = tail

    xpad_sc[0:SUB, :] = tail

    tril = lax.broadcasted_iota(jnp.int32, (CHUNK, CHUNK), 0) >= lax.broadcasted_iota(jnp.int32, (CHUNK, CHUNK), 1)
    lane_lo = lax.broadcasted_iota(jnp.int32, (CHUNK, LANE), 1) < A_HEAD_DIM
    a_neg = -jnp.exp(alog_ref[...])
    gw = A_HEADS // A_GROUPS * A_HEAD_DIM

    def sub_chunk(j, carry):
        r0 = pl.multiple_of(j * CHUNK, CHUNK)
        rows = pl.ds(r0, CHUNK)
        xs = xc_sc[rows, 0:1024]
        bm = xc_sc[rows, 1024:1024 + A_GROUPS * A_STATE]
        cm = xc_sc[rows, 1024 + A_GROUPS * A_STATE:A_CONV_DIM]
        dt = jax.nn.softplus(proj_sc[rows, P0_DT:P0_DT + LANE] + dtb_ref[...])
        da = dt * a_neg
        cs = _mm_exact_rhs(ltri_ref[...], da)
        cs_t = cs.T
        dt_e = _mm_exact_lhs(dt, e64_ref[...])
        cs_e = _mm_exact_lhs(cs, e64_ref[...])
        cs_b = _mm_exact_lhs(cs, e128_ref[...])
        xdt = xs * dt_e
        xdt_bf = xdt.astype(BF16)
        xdec_bf = (xdt * jnp.exp(cs_e[CHUNK - 1:CHUNK, :] - cs_e)).astype(BF16)
        grow = jnp.exp(cs_e)

        y_parts = []
        for g in range(A_GROUPS):
            bg = bm[:, g * A_STATE:(g + 1) * A_STATE].astype(BF16)
            cg = cm[:, g * A_STATE:(g + 1) * A_STATE].astype(BF16)
            cb = lax.dot_general(cg, bg, (((1,), (1,)), ((), ())), preferred_element_type=F32)
            st = state_sc[g * gw:(g + 1) * gw, :]
            y_off = lax.dot_general(cg, st.astype(BF16), (((1,), (1,)), ((), ())), preferred_element_type=F32)
            upd = lax.dot_general(xdec_bf[:, g * gw:(g + 1) * gw], bg, (((0,), (0,)), ((), ())),
                                  preferred_element_type=F32)
            for hp in range(A_HEADS // A_GROUPS // 2):
                res = []
                for q in range(2):
                    h = g * (A_HEADS // A_GROUPS) + 2 * hp + q
                    diff = cs_b[:, h * LANE:(h + 1) * LANE] - cs_t[h:h + 1, :]
                    lm = jnp.exp(jnp.where(tril, diff, -jnp.inf))
                    wm = (cb * lm).astype(BF16)
                    res.append(_mm(wm, xdt_bf[:, g * gw + hp * LANE:g * gw + (hp + 1) * LANE]))
                    r_lo = g * gw + (2 * hp + q) * A_HEAD_DIM
                    state_sc[r_lo:r_lo + A_HEAD_DIM, :] = (
                        jnp.exp(cs_t[h:h + 1, CHUNK - 1:CHUNK]) * st[(2 * hp + q) * A_HEAD_DIM:(2 * hp + q + 1) * A_HEAD_DIM, :]
                        + upd[(2 * hp + q) * A_HEAD_DIM:(2 * hp + q + 1) * A_HEAD_DIM, :])
                y_parts.append(jnp.where(lane_lo, res[0], res[1]) + y_off[:, hp * LANE:(hp + 1) * LANE]
                               * grow[:, g * gw + hp * LANE:g * gw + (hp + 1) * LANE])
        y = jnp.concatenate(y_parts, axis=-1) + xs * dsk_ref[...]
        ya = _group_rms(y * jax.nn.silu(proj_sc[rows, P0_Z:P0_Z + 1024]), gn_ref[...])
        mix_sc[rows, 0:1024] = ya.astype(BF16)

        ug = jax.nn.gelu(proj_sc[rows, P0_U:P0_U + 1024])
        vn = _ln(jax.nn.gelu(proj_sc[rows, P0_V:P0_V + 1024]), lng_ref[...], lnb_ref[...])
        vn_bf = vn.astype(BF16)
        mixed = []
        for g in range(B_GROUPS):
            wt = jnp.where(tril, ws_ref[g], 0.0).astype(BF16)
            mixed.append(_mm(wt, vn_bf[:, g * LANE:(g + 1) * LANE]))
        mixed = jnp.concatenate(mixed, axis=-1) + bsf_ref[...]
        yb = ug * mixed * jax.nn.silu(proj_sc[rows, P0_G:P0_G + 1024])
        mix_sc[rows, 1024:2048] = yb.astype(BF16)
        return carry

    lax.fori_loop(0, T // CHUNK, sub_chunk, 0)

    x1_ref[...] = x_ref[...] + _mm(mix_sc[...], wout_ref[...])

    @pl.when(c == last)
    def _():
        ssm_ref[...] = state_sc[...]


def _layer0_prompt(x, p):
    nb, seq, _ = x.shape
    T = T0
    const = lambda shape: pl.BlockSpec(shape, lambda b, c: (0,) * len(shape), pipeline_mode=pl.Buffered(1))
    in_specs = [
        pl.BlockSpec((None, T, D_MODEL), lambda b, c: (b, c, 0)),
        const((1, D_MODEL)), const((D_MODEL, P0_ALL)), const((A_CONV, A_CONV_DIM)), const((1, A_CONV_DIM)),
        const((1, LANE)), const((1, LANE)), const((1, 1024)), const((1, 1024)),
        const((1, 1024)), const((1, 1024)), const((B_GROUPS, CHUNK, CHUNK)), const((CHUNK, 1024)),
        const((2048, D_MODEL)), const((CHUNK, CHUNK)), const((LANE, 1024)), const((LANE, 2048)),
    ]
    out_shape = (jax.ShapeDtypeStruct((nb, seq, D_MODEL), F32),
                 jax.ShapeDtypeStruct((nb, A_HEADS * A_HEAD_DIM, A_STATE), F32),
                 jax.ShapeDtypeStruct((nb, SUB, A_CONV_DIM), F32))
    out_specs = (pl.BlockSpec((None, T, D_MODEL), lambda b, c: (b, c, 0)),
                 pl.BlockSpec((None, A_HEADS * A_HEAD_DIM, A_STATE), lambda b, c: (b, 0, 0)),
                 pl.BlockSpec((None, SUB, A_CONV_DIM), lambda b, c: (b, 0, 0)))
    scratch = [pltpu.VMEM((T, P0_MAIN), F32), pltpu.VMEM((SUB + T, A_CONV_DIM), F32),
               pltpu.VMEM((T, A_CONV_DIM), F32), pltpu.VMEM((T, 2048), BF16),
               pltpu.VMEM((A_HEADS * A_HEAD_DIM, A_STATE), F32)]
    return pl.pallas_call(
        _layer0_prompt_kernel, out_shape=out_shape, grid=(nb, seq // T),
        in_specs=in_specs, out_specs=out_specs, scratch_shapes=scratch,
        compiler_params=pltpu.CompilerParams(dimension_semantics=("arbitrary", "arbitrary"),
                                             vmem_limit_bytes=VMEM_LIMIT_BYTES),
        name="layer0_prompt",
    )(x, p["ne"], p["win0"], p["ssd_cw"], p["ssd_cb"], p["dtb"], p["alog"], p["dsk"], p["gn"],
      p["lng"], p["lnb"], p["ws"], p["bsf"], p["wout0"], p["ltri"], p["e64"], p["e128"])


def _lru_gates(xc, wa_ref, ba_ref, wx_ref, bx_ref, lam_ref):
    xb = xc.astype(BF16)
    nblk = wa_ref.shape[0]
    w = wa_ref.shape[1]
    r = jnp.concatenate([_mm(xb[:, j * w:(j + 1) * w], wa_ref[j]) for j in range(nblk)], axis=-1)
    i = jnp.concatenate([_mm(xb[:, j * w:(j + 1) * w], wx_ref[j]) for j in range(nblk)], axis=-1)
    r = jax.nn.sigmoid(r + ba_ref[...])
    i = jax.nn.sigmoid(i + bx_ref[...])
    log_a = -LRU_C * r * jax.nn.softplus(-lam_ref[...])
    a = jnp.exp(log_a)
    bt = jnp.sqrt(1.0 - a * a) * (i * xc)
    return a, bt


def _layer1_prompt_kernel(x_ref, no_ref, win_ref, ccw_ref, ccb_ref, clg_ref, clb_ref, lcw_ref, lcb_ref,
                          wa_ref, ba_ref, wx_ref, bx_ref, lam_ref, wout_ref, fn_ref,
                          y_ref, ccv_ref, lcv_ref, lru_ref,
                          cpad_sc, lpad_sc, sgc_sc, sgd_sc, xc_sc, cacc_sc, hs_sc, h_sc, mix_sc):
    c = pl.program_id(0)
    last = pl.num_programs(0) - 1
    nb, T, _ = x_ref.shape
    rows_all = nb * T
    ncol = D_MODEL // LANE

    @pl.when(c == 0)
    def _():
        cpad_sc[:, 0:CPAD, :] = jnp.zeros((nb, CPAD, D_MODEL), F32)
        lpad_sc[:, 0:SUB, :] = jnp.zeros((nb, SUB, D_MODEL), F32)
        h_sc[...] = jnp.zeros_like(h_sc)

    x = x_ref[...].reshape(rows_all, D_MODEL)
    hn = _rms(x, no_ref[...]).astype(BF16)
    glu = _mm(hn, win_ref[:, 0:1024]) * jax.nn.sigmoid(_mm(hn, win_ref[:, 1024:2048]))
    xd = _mm(hn, win_ref[:, 3072:4096])
    for b in range(nb):
        cpad_sc[b, CPAD:CPAD + T, :] = glu[b * T:(b + 1) * T, :]
        lpad_sc[b, SUB:SUB + T, :] = xd[b * T:(b + 1) * T, :]
    sgc_sc[...] = jax.nn.silu(_mm(hn, win_ref[:, 2048:3072]))
    sgd_sc[...] = jax.nn.silu(_mm(hn, win_ref[:, 4096:5120]))

    def per_seq(b, carry):
        rows = pl.ds(pl.multiple_of(b * T, T), T)
        for j in range(ncol):
            lanes = slice(j * LANE, (j + 1) * LANE)
            cacc_sc[rows, lanes] = _dwconv_block(cpad_sc[b, :, lanes], ccw_ref, lanes, C_CONV,
                                                 CPAD - (C_CONV - 1), T)
            xc_sc[rows, lanes] = lcb_ref[:, lanes] + _dwconv_block(lpad_sc[b, :, lanes], lcw_ref, lanes, D_CONV,
                                                                   SUB - (D_CONV - 1), T)
        yc = jax.nn.silu(_ln(cacc_sc[rows, :] + ccb_ref[...], clg_ref[...], clb_ref[...])) * sgc_sc[rows, :]
        mix_sc[rows, 0:1024] = yc.astype(BF16)
        return carry

    lax.fori_loop(0, nb, per_seq, 0)

    a, bt = _lru_gates(xc_sc[...], wa_ref, ba_ref, wx_ref, bx_ref, lam_ref)

    ngrp = T // SUB
    a3 = a.reshape(nb * ngrp, SUB, D_MODEL)
    b3 = bt.reshape(nb * ngrp, SUB, D_MODEL)
    row = lax.broadcasted_iota(jnp.int32, a3.shape, 1)
    for d in (1, 2, 4):
        keep = row >= d
        a_sh = jnp.where(keep, pltpu.roll(a3, d, axis=1), 1.0)
        b_sh = jnp.where(keep, pltpu.roll(b3, d, axis=1), 0.0)
        b3 = a3 * b_sh + b3
        a3 = a3 * a_sh
    a4 = a3.reshape(nb, ngrp, SUB, D_MODEL)
    b4 = b3.reshape(nb, ngrp, SUB, D_MODEL)
    h_in = h_sc[...]
    for g in range(ngrp):
        h = b4[:, g] + a4[:, g] * h_in
        hs_sc[:, g * SUB:(g + 1) * SUB, :] = h
        h_in = jnp.broadcast_to(h[:, SUB - 1:SUB, :], h.shape)
    h_sc[...] = h_in
    mix_sc[:, 1024:2048] = (hs_sc[...].reshape(rows_all, D_MODEL) * sgd_sc[...]).astype(BF16)

    out = x + _mm(mix_sc[...], wout_ref[...])
    y_ref[...] = _rms(out, fn_ref[...]).reshape(nb, T, D_MODEL)

    ctail = cpad_sc[:, T:T + CPAD, :]
    ltail = lpad_sc[:, T:T + SUB, :]

    @pl.when(c == last)
    def _():
        ccv_ref[...] = ctail
        lcv_ref[...] = ltail
        lru_ref[...] = h_in

    cpad_sc[:, 0:CPAD, :] = ctail
    lpad_sc[:, 0:SUB, :] = ltail


def _layer1_prompt(x1, p):
    nb, seq, _ = x1.shape
    T = T1
    rows_all = nb * T
    const = lambda shape: pl.BlockSpec(shape, lambda c: (0,) * len(shape), pipeline_mode=pl.Buffered(1))
    nblk, wblk = p["wa"].shape[0], p["wa"].shape[1]
    in_specs = [
        pl.BlockSpec((nb, T, D_MODEL), lambda c: (0, c, 0)),
        const((1, D_MODEL)), const((D_MODEL, 5120)), const((C_CONV, 1024)), const((1, 1024)), const((1, 1024)),
        const((1, 1024)), const((D_CONV, 1024)), const((1, 1024)),
        const((nblk, wblk, wblk)), const((1, 1024)), const((nblk, wblk, wblk)), const((1, 1024)), const((1, 1024)),
        const((2048, D_MODEL)), const((1, D_MODEL)),
    ]
    out_shape = (jax.ShapeDtypeStruct((nb, seq, D_MODEL), F32),
                 jax.ShapeDtypeStruct((nb, CPAD, 1024), F32),
                 jax.ShapeDtypeStruct((nb, SUB, 1024), F32),
                 jax.ShapeDtypeStruct((nb, SUB, 1024), F32))
    out_specs = (pl.BlockSpec((nb, T, D_MODEL), lambda c: (0, c, 0)),
                 pl.BlockSpec((nb, CPAD, 1024), lambda c: (0, 0, 0)),
                 pl.BlockSpec((nb, SUB, 1024), lambda c: (0, 0, 0)),
                 pl.BlockSpec((nb, SUB, 1024), lambda c: (0, 0, 0)))
    ncol = D_MODEL // LANE
    scratch = [pltpu.VMEM((nb, CPAD + T, 1024), F32), pltpu.VMEM((nb, SUB + T, 1024), F32),
               pltpu.VMEM((rows_all, 1024), F32), pltpu.VMEM((rows_all, 1024), F32), pltpu.VMEM((rows_all, 1024), F32),
               pltpu.VMEM((rows_all, 1024), F32), pltpu.VMEM((nb, T, 1024), F32),
               pltpu.VMEM((nb, SUB, 1024), F32), pltpu.VMEM((rows_all, 2048), BF16)]
    return pl.pallas_call(
        _layer1_prompt_kernel, out_shape=out_shape, grid=(seq // T,),
        in_specs=in_specs, out_specs=out_specs, scratch_shapes=scratch,
        compiler_params=pltpu.CompilerParams(dimension_semantics=("arbitrary",),
                                             vmem_limit_bytes=VMEM_LIMIT_BYTES),
        name="layer1_prompt",
    )(x1, p["no"], p["win1"], p["ccw"], p["ccb"], p["clg"], p["clb"], p["lcw"], p["lcb"],
      p["wa"], p["ba"], p["wx"], p["bx"], p["lam"], p["wout1"], p["fn"])


def _sample0_proj_kernel(x_ref, ne_ref, win_ref, sbuf_ref, cw_ref, cb_ref, dtb_ref, alog_ref, dsk_ref,
                         lng_ref, lnb_ref, w00_ref, b00_ref, e64_ref,
                         nbuf_ref, xdtt_ref, dec_ref, bm_ref, ct_ref, xsd_ref, sz_ref, ybg_ref, vn_ref):
    hn = _rms(x_ref[...], ne_ref[...]).astype(BF16)
    xbc = _mm(hn, win_ref[:, P0_MAIN:P0_ALL])
    acc = cb_ref[...] + cw_ref[A_CONV - 1:A_CONV, :] * xbc
    for k in range(A_CONV - 1):
        acc = acc + cw_ref[k:k + 1, :] * sbuf_ref[:, k, :]
    for k in range(A_CONV - 2):
        nbuf_ref[:, k, :] = sbuf_ref[:, k + 1, :]
    nbuf_ref[:, A_CONV - 2, :] = xbc
    xcv = jax.nn.silu(acc)
    xs = xcv[:, 0:1024]
    nbc = A_GROUPS * A_STATE
    bm_ref[...] = xcv[:, 1024:1024 + nbc]
    ct_ref[...] = xcv[:, 1024 + nbc:A_CONV_DIM].T
    dt = jax.nn.softplus(_mm(hn, win_ref[:, P0_DT:P0_DT + LANE]) + dtb_ref[...])
    dec = jnp.exp(dt * (-jnp.exp(alog_ref[...])))
    xdtt_ref[...] = (xs * _mm_exact_lhs(dt, e64_ref[...])).T.astype(BF16)
    dec_ref[...] = dec
    xsd_ref[...] = xs * dsk_ref[...]
    sz_ref[...] = jax.nn.silu(_mm(hn, win_ref[:, P0_Z:P0_Z + 1024]))
    vn = _ln(jax.nn.gelu(_mm(hn, win_ref[:, P0_V:P0_V + 1024])), lng_ref[...], lnb_ref[...])
    vn_ref[...] = vn
    mixed = w00_ref[...] * vn + b00_ref[...]
    ybg_ref[...] = (jax.nn.gelu(_mm(hn, win_ref[:, P0_U:P0_U + 1024])) * mixed
                    * jax.nn.silu(_mm(hn, win_ref[:, P0_G:P0_G + 1024])))


def _sample0_proj(xs, sbuf, p):
    n = xs.shape[0]
    full = lambda a: pl.BlockSpec(a.shape, lambda i: (0,) * a.ndim)
    args = (xs, p["ne"], p["win0"], sbuf, p["ssd_cw"], p["ssd_cb"], p["dtb"], p["alog"], p["dsk"],
            p["lng"], p["lnb"], p["w00"], p["b00"], p["e64"])
    nbc = A_GROUPS * A_STATE
    shapes = [sbuf.shape, (1024, n), (n, LANE), (n, nbc), (nbc, n), (n, 1024), (n, 1024),
              (n, 1024), (n, 1024)]
    dtypes = [F32, BF16] + [F32] * 7
    return pl.pallas_call(
        _sample0_proj_kernel, grid=(1,),
        out_shape=tuple(jax.ShapeDtypeStruct(s, d) for s, d in zip(shapes, dtypes)),
        in_specs=[full(a) for a in args],
        out_specs=tuple(pl.BlockSpec(s, lambda i, nd=len(s): (0,) * nd) for s in shapes),
        compiler_params=pltpu.CompilerParams(dimension_semantics=("arbitrary",), vmem_limit_bytes=VMEM_LIMIT_BYTES),
        name="sample0_proj",
    )(*args)


def _sample0_state_kernel(dec_ref, h_ref, xt_ref, bm_ref, ct_ref, hn_ref, yt_ref):
    i = pl.program_id(0)
    tb = h_ref.shape[0]
    ntok = bm_ref.shape[0]
    hpg = A_HEADS // A_GROUPS
    gw = hpg * A_HEAD_DIM

    @pl.when(i == 0)
    def _():
        yt_ref[...] = jnp.zeros_like(yt_ref)

    row_id = lax.broadcasted_iota(jnp.int32, (ntok, A_STATE), 0)
    lane_id = lax.broadcasted_iota(jnp.int32, (A_STATE, ntok), 1)
    acc = [jnp.zeros((gw, ntok), F32) for _ in range(A_GROUPS)]
    for b in range(tb):
        tok = i * tb + b
        for g in range(A_GROUPS):
            bmask = jnp.where(row_id == tok, bm_ref[:, g * A_STATE:(g + 1) * A_STATE], 0.0).astype(BF16)
            upd = _mm(xt_ref[g * gw:(g + 1) * gw, :], bmask)
            parts = []
            for h in range(hpg):
                lo = g * gw + h * A_HEAD_DIM
                hn = dec_ref[tok, g * hpg + h] * h_ref[b, lo:lo + A_HEAD_DIM, :] + upd[h * A_HEAD_DIM:(h + 1) * A_HEAD_DIM, :]
                hn_ref[b, lo:lo + A_HEAD_DIM, :] = hn
                parts.append(hn)
            cmask = jnp.where(lane_id == tok, ct_ref[g * A_STATE:(g + 1) * A_STATE, :], 0.0).astype(BF16)
            acc[g] = acc[g] + _mm(jnp.concatenate(parts, axis=0).astype(BF16), cmask)
    for g in range(A_GROUPS):
        yt_ref[g * gw:(g + 1) * gw, :] += acc[g]


def _sample0_state(dec, h0, xdtt, bm, ct):
    n = h0.shape[0]
    tb = SUB
    rows = A_HEADS * A_HEAD_DIM
    blk = pl.BlockSpec((tb, rows, A_STATE), lambda i: (i, 0, 0))
    full = lambda a: pl.BlockSpec(a.shape, lambda i: (0,) * a.ndim)
    return pl.pallas_call(
        _sample0_state_kernel, grid=(n // tb,),
        out_shape=(jax.ShapeDtypeStruct(h0.shape, F32), jax.ShapeDtypeStruct((rows, n), F32)),
        in_specs=[pl.BlockSpec(memory_space=pltpu.SMEM), blk, full(xdtt), full(bm), full(ct)],
        out_specs=(blk, pl.BlockSpec((rows, n), lambda i: (0, 0))),
        compiler_params=pltpu.CompilerParams(dimension_semantics=("arbitrary",), vmem_limit_bytes=VMEM_LIMIT_BYTES),
        name="sample0_state",
    )(dec, h0, xdtt, bm, ct)


def _sample_mid_kernel(yt_ref, xsd_ref, sz_ref, ybg_ref, x_ref, gn_ref, wout0_ref,
                       no_ref, win_ref, lbuf_ref, h0_ref, lcw_ref, lcb_ref,
                       wa_ref, ba_ref, wx_ref, bx_ref, lam_ref,
                       x1_ref, glu_ref, sgc_ref, ydg_ref, nlbuf_ref, hnew_ref):
    ya = _group_rms((yt_ref[...].T + xsd_ref[...]) * sz_ref[...], gn_ref[...])
    mix = jnp.concatenate([ya, ybg_ref[...]], axis=-1).astype(BF16)
    x1 = x_ref[...] + _mm(mix, wout0_ref[...])
    x1_ref[...] = x1
    hn = _rms(x1, no_ref[...]).astype(BF16)
    glu_ref[...] = _mm(hn, win_ref[:, 0:1024]) * jax.nn.sigmoid(_mm(hn, win_ref[:, 1024:2048]))
    sgc_ref[...] = jax.nn.silu(_mm(hn, win_ref[:, 2048:3072]))
    xd = _mm(hn, win_ref[:, 3072:4096])
    acc = lcb_ref[...] + lcw_ref[D_CONV - 1:D_CONV, :] * xd
    for k in range(D_CONV - 1):
        acc = acc + lcw_ref[k:k + 1, :] * lbuf_ref[:, k, :]
    for k in range(D_CONV - 2):
        nlbuf_ref[:, k, :] = lbuf_ref[:, k + 1, :]
    nlbuf_ref[:, D_CONV - 2, :] = xd
    a, bt = _lru_gates(acc, wa_ref, ba_ref, wx_ref, bx_ref, lam_ref)
    h = a * h0_ref[...] + bt
    hnew_ref[...] = h
    ydg_ref[...] = h * jax.nn.silu(_mm(hn, win_ref[:, 4096:5120]))


def _sample_mid(yt, xsd, sz, ybg, xs, lbuf, h0, p):
    n = xs.shape[0]
    full = lambda a: pl.BlockSpec(a.shape, lambda i: (0,) * a.ndim)
    args = (yt, xsd, sz, ybg, xs, p["gn"], p["wout0"], p["no"], p["win1"], lbuf, h0, p["lcw"], p["lcb"],
            p["wa"], p["ba"], p["wx"], p["bx"], p["lam"])
    shapes = [(n, 1024), (n, 1024), (n, 1024), (n, 1024), lbuf.shape, (n, 1024)]
    return pl.pallas_call(
        _sample_mid_kernel, grid=(1,),
        out_shape=tuple(jax.ShapeDtypeStruct(s, F32) for s in shapes),
        in_specs=[full(a) for a in args],
        out_specs=tuple(pl.BlockSpec(s, lambda i, nd=len(s): (0,) * nd) for s in shapes),
        compiler_params=pltpu.CompilerParams(dimension_semantics=("arbitrary",), vmem_limit_bytes=VMEM_LIMIT_BYTES),
        name="sample_mid",
    )(*args)


def _sample1_tail_kernel(cbuf_ref, glu_ref, sgc_ref, ydg_ref, x1_ref, ccw_ref, ccb_ref, clg_ref, clb_ref,
                         wout_ref, fn_ref, ncbuf_ref, y_ref, acc_sc):
    tb = cbuf_ref.shape[0]
    hist = C_CONV - 1
    w_hist = ccw_ref[0:hist, :]
    for b in range(tb):
        acc_sc[b:b + 1, :] = jnp.sum(cbuf_ref[b] * w_hist, axis=0, keepdims=True)
        ncbuf_ref[b, 0:hist - 1, :] = cbuf_ref[b, 1:hist, :]
        ncbuf_ref[b, hist - 1:hist, :] = glu_ref[b:b + 1, :]
    acc = acc_sc[...] + ccb_ref[...] + ccw_ref[hist:C_CONV, :] * glu_ref[...]
    yc = jax.nn.silu(_ln(acc, clg_ref[...], clb_ref[...])) * sgc_ref[...]
    mix = jnp.concatenate([yc, ydg_ref[...]], axis=-1).astype(BF16)
    out = x1_ref[...] + _mm(mix, wout_ref[...])
    y_ref[...] = _rms(out, fn_ref[...])


def _sample1_tail(cbuf, glu, sgc, ydg, x1, p):
    n = x1.shape[0]
    tb = 16
    row = lambda w: pl.BlockSpec((tb, w), lambda i: (i, 0))
    blk = pl.BlockSpec((tb,) + cbuf.shape[1:], lambda i: (i, 0, 0))
    const = lambda a: pl.BlockSpec(a.shape, lambda i: (0,) * a.ndim, pipeline_mode=pl.Buffered(1))
    consts = (p["ccw"], p["ccb"], p["clg"], p["clb"], p["wout1"], p["fn"])
    return pl.pallas_call(
        _sample1_tail_kernel, grid=(n // tb,),
        out_shape=(jax.ShapeDtypeStruct(cbuf.shape, F32), jax.ShapeDtypeStruct((n, 1024), F32)),
        in_specs=[blk, row(1024), row(1024), row(1024), row(1024)] + [const(a) for a in consts],
        out_specs=(blk, row(1024)),
        scratch_shapes=[pltpu.VMEM((tb, 1024), F32)],
        compiler_params=pltpu.CompilerParams(dimension_semantics=("arbitrary",), vmem_limit_bytes=VMEM_LIMIT_BYTES),
        name="sample1_tail",
    )(cbuf, glu, sgc, ydg, x1, *consts)


def _block_diag(w, per_block):
    nh, d, _ = w.shape
    nblk = nh // per_block
    w = w.reshape(nblk, per_block, d, d)
    eye = jnp.eye(per_block, dtype=w.dtype)
    bd = jnp.einsum("nhij,hk->nhikj", w, eye)
    return bd.reshape(nblk, per_block * d, per_block * d).astype(BF16)


def _prepare(norm_even, w_in_even, ssd_conv_w, ssd_conv_b, ssd_dt_bias, ssd_a_log, ssd_d, ssd_norm,
             gmlp_ln_g, gmlp_ln_b, gmlp_w_s, gmlp_b_s, w_out_even,
             norm_odd, w_in_odd, ccv_w, ccv_b, ccv_ln_g, ccv_ln_b,
             lru_conv_w, lru_conv_b, lru_wa, lru_ba, lru_wx, lru_bx, lru_lambda, w_out_odd, final_norm):
    row = lambda v: v.reshape(1, -1).astype(F32)
    w0 = w_in_even[0]
    z, xbc, dtw, u, v, g = jnp.split(w0, [1024, 2560, 2576, 3600, 4624], axis=-1)
    dtw = jnp.pad(dtw, ((0, 0), (0, LANE - A_HEADS)))
    heads = jnp.arange(LANE)[:, None]
    p = {
        "ne": row(norm_even[0]),
        "win0": jnp.concatenate([z, u, v, g, dtw, xbc], axis=-1).astype(BF16),
        "ssd_cw": ssd_conv_w[0], "ssd_cb": row(ssd_conv_b[0]),
        "dtb": jnp.pad(row(ssd_dt_bias[0]), ((0, 0), (0, LANE - A_HEADS))),
        "alog": jnp.pad(row(ssd_a_log[0]), ((0, 0), (0, LANE - A_HEADS))),
        "dsk": row(jnp.repeat(ssd_d[0], A_HEAD_DIM)),
        "gn": row(ssd_norm[0]),
        "lng": row(gmlp_ln_g[0]), "lnb": row(gmlp_ln_b[0]),
        "ws": gmlp_w_s[0],
        "bsf": jnp.repeat(gmlp_b_s[0].T, LANE, axis=1),
        "w00": row(jnp.repeat(gmlp_w_s[0, :, 0, 0], LANE)), "b00": row(jnp.repeat(gmlp_b_s[0, :, 0], LANE)),
        "wout0": w_out_even[0].astype(BF16),
        "ltri": (jnp.arange(CHUNK)[:, None] >= jnp.arange(CHUNK)[None, :]).astype(BF16),
        "e64": (jnp.arange(1024)[None, :] // A_HEAD_DIM == heads).astype(BF16),
        "e128": (jnp.arange(2048)[None, :] // LANE == heads).astype(BF16),
        "no": row(norm_odd[0]),
        "win1": w_in_odd[0].astype(BF16),
        "ccw": ccv_w[0], "ccb": row(ccv_b[0]), "clg": row(ccv_ln_g[0]), "clb": row(ccv_ln_b[0]),
        "lcw": lru_conv_w[0], "lcb": row(lru_conv_b[0]),
        "wa": _block_diag(lru_wa[0], 4), "ba": row(lru_ba[0]),
        "wx": _block_diag(lru_wx[0], 4), "bx": row(lru_bx[0]),
        "lam": row(lru_lambda[0]),
        "wout1": w_out_odd[0].astype(BF16),
        "fn": row(final_norm),
    }
    return p


def kernel(x_prompt, x_sample, state_ssm, state_ssd_conv, state_ccv, state_lru_conv, state_lru, norm_even, w_in_even, ssd_conv_w, ssd_conv_b, ssd_dt_bias, ssd_a_log, ssd_d, ssd_norm, gmlp_ln_g, gmlp_ln_b, gmlp_w_s, gmlp_b_s, w_out_even, norm_odd, w_in_odd, ccv_w, ccv_b, ccv_ln_g, ccv_ln_b, lru_conv_w, lru_conv_b, lru_wa, lru_ba, lru_wx, lru_bx, lru_lambda, w_out_odd, final_norm):
    p = _prepare(norm_even, w_in_even, ssd_conv_w, ssd_conv_b, ssd_dt_bias, ssd_a_log, ssd_d, ssd_norm,
                 gmlp_ln_g, gmlp_ln_b, gmlp_w_s, gmlp_b_s, w_out_even,
                 norm_odd, w_in_odd, ccv_w, ccv_b, ccv_ln_g, ccv_ln_b,
                 lru_conv_w, lru_conv_b, lru_wa, lru_ba, lru_wx, lru_bx, lru_lambda, w_out_odd, final_norm)
    nb = x_prompt.shape[0]
    ns = x_sample.shape[0]

    x1, ssm_p, sbuf_p = _layer0_prompt(x_prompt, p)
    y_prompt, cbuf_p, lbuf_p, lru_p = _layer1_prompt(x1, p)

    xs = x_sample.reshape(ns, D_MODEL)
    nsbuf, xdtt, dec, bm, ct, xsd, sz, ybg, vn = _sample0_proj(xs, state_ssd_conv[0], p)
    ssm_s, y_ssd = _sample0_state(dec[:, :A_HEADS], state_ssm[0].reshape(ns, A_HEADS * A_HEAD_DIM, A_STATE),
                                  xdtt, bm, ct)
    x1s, glu, sgc, ydg, nlbuf, lru_s = _sample_mid(y_ssd, xsd, sz, ybg, xs, state_lru_conv[0], state_lru[0], p)
    ncbuf, y_sample = _sample1_tail(state_ccv[0], glu, sgc, ydg, x1s, p)

    return (y_prompt,
            y_sample.reshape(ns, 1, D_MODEL),
            ssm_p.reshape(1, nb, A_HEADS, A_HEAD_DIM, A_STATE),
            ssm_s.reshape(1, ns, A_HEADS, A_HEAD_DIM, A_STATE),
            sbuf_p[:, SUB - (A_CONV - 1):, :][None],
            nsbuf[None],
            vn.reshape(1, ns, 1, 1024),
            cbuf_p[:, CPAD - (C_CONV - 1):, :][None],
            ncbuf[None],
            lbuf_p[:, SUB - (D_CONV - 1):, :][None],
            nlbuf[None],
            lru_p[:, 0, :][None],
            lru_s[None])
```

```python
import functools

import jax
import jax.numpy as jnp
from jax import lax
from jax.experimental import pallas as pl
from jax.experimental.pallas import tpu as pltpu

F32 = jnp.float32
BF16 = jnp.bfloat16

D_MODEL = 1024
EPS = 1e-6
A_HEADS = 16
A_HEAD_DIM = 64
A_STATE = 128
A_GROUPS = 2
A_CONV = 4
A_CONV_DIM = 1536
CHUNK = 128
B_GROUPS = 8
C_CONV = 31
D_CONV = 4
D_HEADS = 16
D_BLOCK = 64
LRU_C = 8.0

P0_Z, P0_U, P0_V, P0_G, P0_DT, P0_XBC = 0, 1024, 2048, 3072, 4096, 4224
P0_MAIN = 4224
P0_ALL = P0_MAIN + A_CONV_DIM

VMEM_LIMIT_BYTES = 56 * 1024 * 1024

T0 = 512
T1 = 64
LANE = 128
SUB = 8
CPAD = 32


def _mm(a, b):
    return jnp.dot(a, b, preferred_element_type=F32)


def _rms(x, g):
    return x * lax.rsqrt(jnp.mean(x * x, axis=-1, keepdims=True) + EPS) * g


def _ln(x, g, b):
    mu = jnp.mean(x, axis=-1, keepdims=True)
    xc = x - mu
    var = jnp.mean(xc * xc, axis=-1, keepdims=True)
    return xc * lax.rsqrt(var + EPS) * g + b


def _split3(v):
    hi = v.astype(BF16)
    r1 = v - hi.astype(F32)
    mid = r1.astype(BF16)
    lo = (r1 - mid.astype(F32)).astype(BF16)
    return hi, mid, lo


def _mm_exact_rhs(m01, v):
    hi, mid, lo = _split3(v)
    return _mm(m01, hi) + _mm(m01, mid) + _mm(m01, lo)


def _mm_exact_lhs(v, m01):
    hi, mid, lo = _split3(v)
    return _mm(hi, m01) + _mm(mid, m01) + _mm(lo, m01)


def _mm_wide_lhs(v, m01):
    hi = v.astype(BF16)
    mid = (v - hi.astype(F32)).astype(BF16)
    return _mm(hi, m01) + _mm(mid, m01)


def _group_rms(y, gn):
    half = y.shape[-1] // A_GROUPS
    parts = []
    for g in range(A_GROUPS):
        yg = y[:, g * half:(g + 1) * half]
        parts.append(yg * lax.rsqrt(jnp.mean(yg * yg, axis=-1, keepdims=True) + EPS))
    return jnp.concatenate(parts, axis=-1) * gn


def _dwconv_block(xf, w_ref, lanes, n_taps, first_off, t_rows):
    n = xf.shape[0]
    acc = None
    for r in range(SUB):
        taps = [(off // SUB, off - first_off) for off in range(first_off, first_off + n_taps) if off % SUB == r]
        if not taps:
            continue
        xr = xf if r == 0 else pltpu.roll(xf, n - r, axis=0)
        for q, k in taps:
            term = w_ref[k:k + 1, lanes] * xr[SUB * q:SUB * q + t_rows]
            acc = term if acc is None else acc + term
    return acc


def _layer0_prompt_kernel(x_ref, ne_ref, win_ref, cw_ref, cb_ref, dtb_ref, alog_ref, dsk_ref, gn_ref,
                          lng_ref, lnb_ref, ws_ref, bsf_ref, wout_ref, ltri_ref, e64_ref, e128_ref,
                          x1_ref, ssm_ref, cbuf_ref,
                          proj_sc, xpad_sc, xc_sc, mix_sc, state_sc):
    c = pl.program_id(1)
    last = pl.num_programs(1) - 1
    T = x_ref.shape[0]

    @pl.when(c == 0)
    def _():
        state_sc[...] = jnp.zeros_like(state_sc)
        xpad_sc[0:SUB, :] = jnp.zeros((SUB, A_CONV_DIM), F32)

    hn = _rms(x_ref[...], ne_ref[...]).astype(BF16)
    for s in range(0, P0_MAIN, 512):
        e = min(s + 512, P0_MAIN)
        proj_sc[:, s:e] = _mm(hn, win_ref[:, s:e])
    for s in range(0, A_CONV_DIM, 512):
        xpad_sc[SUB:SUB + T, s:s + 512] = _mm(hn, win_ref[:, P0_MAIN + s:P0_MAIN + s + 512])

    acc = cb_ref[...] + cw_ref[A_CONV - 1:A_CONV, :] * xpad_sc[SUB:SUB + T, :]
    for k in range(A_CONV - 1):
        off = SUB - (A_CONV - 1) + k
        acc = acc + cw_ref[k:k + 1, :] * xpad_sc[off:off + T, :]
    xc_sc[...] = jax.nn.silu(acc)
    tail = xpad_sc[T:T + SUB, :]

    @pl.when(c == last)
    def _():
        cbuf_ref[...] = tail

    xpad_sc[0:SUB, :] = tail

    tril = lax.broadcasted_iota(jnp.int32, (CHUNK, CHUNK), 0) >= lax.broadcasted_iota(jnp.int32, (CHUNK, CHUNK), 1)
    lane_lo = lax.broadcasted_iota(jnp.int32, (CHUNK, LANE), 1) < A_HEAD_DIM
    a_neg = -jnp.exp(alog_ref[...])
    gw = A_HEADS // A_GROUPS * A_HEAD_DIM

    for j in range(T // CHUNK):
        rows = slice(j * CHUNK, (j + 1) * CHUNK)
        xs = xc_sc[rows, 0:1024]
        bm = xc_sc[rows, 1024:1024 + A_GROUPS * A_STATE]
        cm = xc_sc[rows, 1024 + A_GROUPS * A_STATE:A_CONV_DIM]
        dt = jax.nn.softplus(proj_sc[rows, P0_DT:P0_DT + LANE] + dtb_ref[...])
        da = dt * a_neg
        cs = _mm_exact_rhs(ltri_ref[...], da)
        cs_t = cs.T
        dt_e = _mm_wide_lhs(dt, e64_ref[...])
        cs_b = _mm_wide_lhs(cs, e128_ref[...])
        cs_e = jnp.concatenate([jnp.where(lane_lo, cs_b[:, 2 * q * LANE:(2 * q + 1) * LANE],
                                          cs_b[:, (2 * q + 1) * LANE:(2 * q + 2) * LANE])
                                for q in range(A_HEADS // 2)], axis=-1)
        xdt = xs * dt_e
        xdt_bf = xdt.astype(BF16)
        xdec_bf = (xdt * jnp.exp(cs_e[CHUNK - 1:CHUNK, :] - cs_e)).astype(BF16)
        grow = jnp.exp(cs_e)

        y_parts = []
        for g in range(A_GROUPS):
            bg = bm[:, g * A_STATE:(g + 1) * A_STATE].astype(BF16)
            cg = cm[:, g * A_STATE:(g + 1) * A_STATE].astype(BF16)
            cb = lax.dot_general(cg, bg, (((1,), (1,)), ((), ())), preferred_element_type=F32)
            st = state_sc[g * gw:(g + 1) * gw, :]
            y_off = lax.dot_general(cg, st.astype(BF16), (((1,), (1,)), ((), ())), preferred_element_type=F32)
            upd = lax.dot_general(xdec_bf[:, g * gw:(g + 1) * gw], bg, (((0,), (0,)), ((), ())),
                                  preferred_element_type=F32)
            for hp in range(A_HEADS // A_GROUPS // 2):
                res = []
                for q in range(2):
                    h = g * (A_HEADS // A_GROUPS) + 2 * hp + q
                    diff = cs_b[:, h * LANE:(h + 1) * LANE] - cs_t[h:h + 1, :]
                    lm = jnp.exp(jnp.where(tril, diff, -jnp.inf))
                    wm = (cb * lm).astype(BF16)
                    res.append(_mm(wm, xdt_bf[:, g * gw + hp * LANE:g * gw + (hp + 1) * LANE]))
                    r_lo = g * gw + (2 * hp + q) * A_HEAD_DIM
                    state_sc[r_lo:r_lo + A_HEAD_DIM, :] = (
                        jnp.exp(cs_t[h:h + 1, CHUNK - 1:CHUNK]) * st[(2 * hp + q) * A_HEAD_DIM:(2 * hp + q + 1) * A_HEAD_DIM, :]
                        + upd[(2 * hp + q) * A_HEAD_DIM:(2 * hp + q + 1) * A_HEAD_DIM, :])
                y_parts.append(jnp.where(lane_lo, res[0], res[1]) + y_off[:, hp * LANE:(hp + 1) * LANE]
                               * grow[:, g * gw + hp * LANE:g * gw + (hp + 1) * LANE])
        y = jnp.concatenate(y_parts, axis=-1) + xs * dsk_ref[...]
        ya = _group_rms(y * jax.nn.silu(proj_sc[rows, P0_Z:P0_Z + 1024]), gn_ref[...])
        mix_sc[rows, 0:1024] = ya.astype(BF16)

        ug = jax.nn.gelu(proj_sc[rows, P0_U:P0_U + 1024])
        vn = _ln(jax.nn.gelu(proj_sc[rows, P0_V:P0_V + 1024]), lng_ref[...], lnb_ref[...])
        vn_bf = vn.astype(BF16)
        mixed = []
        for g in range(B_GROUPS):
            wt = jnp.where(tril, ws_ref[g], 0.0).astype(BF16)
            mixed.append(_mm(wt, vn_bf[:, g * LANE:(g + 1) * LANE]))
        mixed = jnp.concatenate(mixed, axis=-1) + bsf_ref[...]
        yb = ug * mixed * jax.nn.silu(proj_sc[rows, P0_G:P0_G + 1024])
        mix_sc[rows, 1024:2048] = yb.astype(BF16)

    x1_ref[...] = x_ref[...] + _mm(mix_sc[...], wout_ref[...])

    @pl.when(c == last)
    def _():
        ssm_ref[...] = state_sc[...]


def _layer0_prompt(x, p):
    nb, seq, _ = x.shape
    T = T0
    const = lambda shape: pl.BlockSpec(shape, lambda b, c: (0,) * len(shape), pipeline_mode=pl.Buffered(1))
    in_specs = [
        pl.BlockSpec((None, T, D_MODEL), lambda b, c: (b, c, 0)),
        const((1, D_MODEL)), const((D_MODEL, P0_ALL)), const((A_CONV, A_CONV_DIM)), const((1, A_CONV_DIM)),
        const((1, LANE)), const((1, LANE)), const((1, 1024)), const((1, 1024)),
        const((1, 1024)), const((1, 1024)), const((B_GROUPS, CHUNK, CHUNK)), const((CHUNK, 1024)),
        const((2048, D_MODEL)), const((CHUNK, CHUNK)), const((LANE, 1024)), const((LANE, 2048)),
    ]
    out_shape = (jax.ShapeDtypeStruct((nb, seq, D_MODEL), F32),
                 jax.ShapeDtypeStruct((nb, A_HEADS * A_HEAD_DIM, A_STATE), F32),
                 jax.ShapeDtypeStruct((nb, SUB, A_CONV_DIM), F32))
    out_specs = (pl.BlockSpec((None, T, D_MODEL), lambda b, c: (b, c, 0)),
                 pl.BlockSpec((None, A_HEADS * A_HEAD_DIM, A_STATE), lambda b, c: (b, 0, 0)),
                 pl.BlockSpec((None, SUB, A_CONV_DIM), lambda b, c: (b, 0, 0)))
    scratch = [pltpu.VMEM((T, P0_MAIN), F32), pltpu.VMEM((SUB + T, A_CONV_DIM), F32),
               pltpu.VMEM((T, A_CONV_DIM), F32), pltpu.VMEM((T, 2048), BF16),
               pltpu.VMEM((A_HEADS * A_HEAD_DIM, A_STATE), F32)]
    return pl.pallas_call(
        _layer0_prompt_kernel, out_shape=out_shape, grid=(nb, seq // T),
        in_specs=in_specs, out_specs=out_specs, scratch_shapes=scratch,
        compiler_params=pltpu.CompilerParams(dimension_semantics=("arbitrary", "arbitrary"),
                                             vmem_limit_bytes=VMEM_LIMIT_BYTES),
        name="layer0_prompt",
    )(x, p["ne"], p["win0"], p["ssd_cw"], p["ssd_cb"], p["dtb"], p["alog"], p["dsk"], p["gn"],
      p["lng"], p["lnb"], p["ws"], p["bsf"], p["wout0"], p["ltri"], p["e64"], p["e128"])


def _lru_gates(xc, wa_ref, ba_ref, wx_ref, bx_ref, lam_ref):
    xb = xc.astype(BF16)
    nblk = wa_ref.shape[0]
    w = wa_ref.shape[1]
    r = jnp.concatenate([_mm(xb[:, j * w:(j + 1) * w], wa_ref[j]) for j in range(nblk)], axis=-1)
    i = jnp.concatenate([_mm(xb[:, j * w:(j + 1) * w], wx_ref[j]) for j in range(nblk)], axis=-1)
    r = jax.nn.sigmoid(r + ba_ref[...])
    i = jax.nn.sigmoid(i + bx_ref[...])
    log_a = -LRU_C * r * jax.nn.softplus(-lam_ref[...])
    a = jnp.exp(log_a)
    bt = jnp.sqrt(1.0 - a * a) * (i * xc)
    return a, bt


def _layer1_prompt_kernel(x_ref, no_ref, win_ref, ccw_ref, ccb_ref, clg_ref, clb_ref, lcw_ref, lcb_ref,
                          wa_ref, ba_ref, wx_ref, bx_ref, lam_ref, wout_ref, fn_ref,
                          y_ref, ccv_ref, lcv_ref, lru_ref,
                          cpad_sc, lpad_sc, sgc_sc, sgd_sc, xc_sc, cacc_sc, hs_sc, h_sc, mix_sc):
    c = pl.program_id(0)
    last = pl.num_programs(0) - 1
    nb, T, _ = x_ref.shape
    rows_all = nb * T
    ncol = D_MODEL // LANE

    @pl.when(c == 0)
    def _():
        cpad_sc[:, 0:CPAD, :] = jnp.zeros((nb, CPAD, D_MODEL), F32)
        lpad_sc[:, 0:SUB, :] = jnp.zeros((nb, SUB, D_MODEL), F32)
        h_sc[...] = jnp.zeros_like(h_sc)

    x = x_ref[...].reshape(rows_all, D_MODEL)
    hn = _rms(x, no_ref[...]).astype(BF16)
    glu = _mm(hn, win_ref[:, 0:1024]) * jax.nn.sigmoid(_mm(hn, win_ref[:, 1024:2048]))
    xd = _mm(hn, win_ref[:, 3072:4096])
    for b in range(nb):
        cpad_sc[b, CPAD:CPAD + T, :] = glu[b * T:(b + 1) * T, :]
        lpad_sc[b, SUB:SUB + T, :] = xd[b * T:(b + 1) * T, :]
    sgc_sc[...] = jax.nn.silu(_mm(hn, win_ref[:, 2048:3072]))
    sgd_sc[...] = jax.nn.silu(_mm(hn, win_ref[:, 4096:5120]))

    for b in range(nb):
        rows = slice(b * T, (b + 1) * T)
        for j in range(ncol):
            lanes = slice(j * LANE, (j + 1) * LANE)
            cacc_sc[rows, lanes] = _dwconv_block(cpad_sc[b, :, lanes], ccw_ref, lanes, C_CONV,
                                                 CPAD - (C_CONV - 1), T)
            xc_sc[rows, lanes] = lcb_ref[:, lanes] + _dwconv_block(lpad_sc[b, :, lanes], lcw_ref, lanes, D_CONV,
                                                                   SUB - (D_CONV - 1), T)
        yc = jax.nn.silu(_ln(cacc_sc[rows, :] + ccb_ref[...], clg_ref[...], clb_ref[...])) * sgc_sc[rows, :]
        mix_sc[rows, 0:1024] = yc.astype(BF16)

    a, bt = _lru_gates(xc_sc[...], wa_ref, ba_ref, wx_ref, bx_ref, lam_ref)

    ngrp = T // SUB
    a3 = a.reshape(nb * ngrp, SUB, D_MODEL)
    b3 = bt.reshape(nb * ngrp, SUB, D_MODEL)
    row = lax.broadcasted_iota(jnp.int32, a3.shape, 1)
    for d in (1, 2, 4):
        keep = row >= d
        a_sh = jnp.where(keep, pltpu.roll(a3, d, axis=1), 1.0)
        b_sh = jnp.where(keep, pltpu.roll(b3, d, axis=1), 0.0)
        b3 = a3 * b_sh + b3
        a3 = a3 * a_sh
    a4 = a3.reshape(nb, ngrp, SUB, D_MODEL)
    b4 = b3.reshape(nb, ngrp, SUB, D_MODEL)
    h_in = h_sc[...]
    for g in range(ngrp):
        h = b4[:, g] + a4[:, g] * h_in
        hs_sc[:, g * SUB:(g + 1) * SUB, :] = h
        h_in = jnp.broadcast_to(h[:, SUB - 1:SUB, :], h.shape)
    h_sc[...] = h_in
    mix_sc[:, 1024:2048] = (hs_sc[...].reshape(rows_all, D_MODEL) * sgd_sc[...]).astype(BF16)

    out = x + _mm(mix_sc[...], wout_ref[...])
    y_ref[...] = _rms(out, fn_ref[...]).reshape(nb, T, D_MODEL)

    ctail = cpad_sc[:, T:T + CPAD, :]
    ltail = lpad_sc[:, T:T + SUB, :]

    @pl.when(c == last)
    def _():
        ccv_ref[...] = ctail
        lcv_ref[...] = ltail
        lru_ref[...] = h_in

    cpad_sc[:, 0:CPAD, :] = ctail
    lpad_sc[:, 0:SUB, :] = ltail


def _layer1_prompt(x1, p):
    nb, seq, _ = x1.shape
    T = T1
    rows_all = nb * T
    const = lambda shape: pl.BlockSpec(shape, lambda c: (0,) * len(shape), pipeline_mode=pl.Buffered(1))
    nblk, wblk = p["wa"].shape[0], p["wa"].shape[1]
    in_specs = [
        pl.BlockSpec((nb, T, D_MODEL), lambda c: (0, c, 0)),
        const((1, D_MODEL)), const((D_MODEL, 5120)), const((C_CONV, 1024)), const((1, 1024)), const((1, 1024)),
        const((1, 1024)), const((D_CONV, 1024)), const((1, 1024)),
        const((nblk, wblk, wblk)), const((1, 1024)), const((nblk, wblk, wblk)), const((1, 1024)), const((1, 1024)),
        const((2048, D_MODEL)), const((1, D_MODEL)),
    ]
    out_shape = (jax.ShapeDtypeStruct((nb, seq, D_MODEL), F32),
                 jax.ShapeDtypeStruct((nb, CPAD, 1024), F32),
                 jax.ShapeDtypeStruct((nb, SUB, 1024), F32),
                 jax.ShapeDtypeStruct((nb, SUB, 1024), F32))
    out_specs = (pl.BlockSpec((nb, T, D_MODEL), lambda c: (0, c, 0)),
                 pl.BlockSpec((nb, CPAD, 1024), lambda c: (0, 0, 0)),
                 pl.BlockSpec((nb, SUB, 1024), lambda c: (0, 0, 0)),
                 pl.BlockSpec((nb, SUB, 1024), lambda c: (0, 0, 0)))
    ncol = D_MODEL // LANE
    scratch = [pltpu.VMEM((nb, CPAD + T, 1024), F32), pltpu.VMEM((nb, SUB + T, 1024), F32),
               pltpu.VMEM((rows_all, 1024), F32), pltpu.VMEM((rows_all, 1024), F32), pltpu.VMEM((rows_all, 1024), F32),
               pltpu.VMEM((rows_all, 1024), F32), pltpu.VMEM((nb, T, 1024), F32),
               pltpu.VMEM((nb, SUB, 1024), F32), pltpu.VMEM((rows_all, 2048), BF16)]
    return pl.pallas_call(
        _layer1_prompt_kernel, out_shape=out_shape, grid=(seq // T,),
        in_specs=in_specs, out_specs=out_specs, scratch_shapes=scratch,
        compiler_params=pltpu.CompilerParams(dimension_semantics=("arbitrary",),
                                             vmem_limit_bytes=VMEM_LIMIT_BYTES),
        name="layer1_prompt",
    )(x1, p["no"], p["win1"], p["ccw"], p["ccb"], p["clg"], p["clb"], p["lcw"], p["lcb"],
      p["wa"], p["ba"], p["wx"], p["bx"], p["lam"], p["wout1"], p["fn"])


def _sample0_proj_kernel(x_ref, ne_ref, win_ref, sbuf_ref, cw_ref, cb_ref, dtb_ref, alog_ref, dsk_ref,
                         lng_ref, lnb_ref, w00_ref, b00_ref, e64_ref,
                         nbuf_ref, xdtt_ref, dec_ref, bm_ref, ct_ref, xsd_ref, sz_ref, ybg_ref, vn_ref):
    hn = _rms(x_ref[...], ne_ref[...]).astype(BF16)
    xbc = _mm(hn, win_ref[:, P0_MAIN:P0_ALL])
    acc = cb_ref[...] + cw_ref[A_CONV - 1:A_CONV, :] * xbc
    for k in range(A_CONV - 1):
        acc = acc + cw_ref[k:k + 1, :] * sbuf_ref[k]
    for k in range(A_CONV - 2):
        nbuf_ref[k] = sbuf_ref[k + 1]
    nbuf_ref[A_CONV - 2] = xbc
    xcv = jax.nn.silu(acc)
    xs = xcv[:, 0:1024]
    nbc = A_GROUPS * A_STATE
    bm_ref[...] = xcv[:, 1024:1024 + nbc]
    ct_ref[...] = xcv[:, 1024 + nbc:A_CONV_DIM].T
    dt = jax.nn.softplus(_mm(hn, win_ref[:, P0_DT:P0_DT + LANE]) + dtb_ref[...])
    dec = jnp.exp(dt * (-jnp.exp(alog_ref[...])))
    xdtt_ref[...] = (xs * _mm_exact_lhs(dt, e64_ref[...])).T.astype(BF16)
    dec_ref[...] = dec
    xsd_ref[...] = xs * dsk_ref[...]
    sz_ref[...] = jax.nn.silu(_mm(hn, win_ref[:, P0_Z:P0_Z + 1024]))
    vn = _ln(jax.nn.gelu(_mm(hn, win_ref[:, P0_V:P0_V + 1024])), lng_ref[...], lnb_ref[...])
    vn_ref[...] = vn
    mixed = w00_ref[...] * vn + b00_ref[...]
    ybg_ref[...] = (jax.nn.gelu(_mm(hn, win_ref[:, P0_U:P0_U + 1024])) * mixed
                    * jax.nn.silu(_mm(hn, win_ref[:, P0_G:P0_G + 1024])))


def _sample0_proj(xs, sbuf, p):
    n = xs.shape[0]
    full = lambda a: pl.BlockSpec(a.shape, lambda i: (0,) * a.ndim)
    args = (xs, p["ne"], p["win0"], sbuf, p["ssd_cw"], p["ssd_cb"], p["dtb"], p["alog"], p["dsk"],
            p["lng"], p["lnb"], p["w00"], p["b00"], p["e64"])
    nbc = A_GROUPS * A_STATE
    shapes = [sbuf.shape, (1024, n), (n, LANE), (n, nbc), (nbc, n), (n, 1024), (n, 1024),
              (n, 1024), (n, 1024)]
    dtypes = [F32, BF16] + [F32] * 7
    return pl.pallas_call(
        _sample0_proj_kernel, grid=(1,),
        out_shape=tuple(jax.ShapeDtypeStruct(s, d) for s, d in zip(shapes, dtypes)),
        in_specs=[full(a) for a in args],
        out_specs=tuple(pl.BlockSpec(s, lambda i, nd=len(s): (0,) * nd) for s in shapes),
        compiler_params=pltpu.CompilerParams(dimension_semantics=("arbitrary",), vmem_limit_bytes=VMEM_LIMIT_BYTES),
        name="sample0_proj",
    )(*args)


def _sample0_state_kernel(dec_ref, h_ref, xt_ref, bm_ref, ct_ref, hn_ref, yt_ref):
    i = pl.program_id(0)
    tb = h_ref.shape[0]
    ntok = bm_ref.shape[0]
    hpg = A_HEADS // A_GROUPS
    gw = hpg * A_HEAD_DIM

    @pl.when(i == 0)
    def _():
        yt_ref[...] = jnp.zeros_like(yt_ref)

    row_id = lax.broadcasted_iota(jnp.int32, (ntok, A_STATE), 0)
    lane_id = lax.broadcasted_iota(jnp.int32, (A_STATE, ntok), 1)
    acc = [jnp.zeros((gw, ntok), F32) for _ in range(A_GROUPS)]
    for b in range(tb):
        tok = i * tb + b
        for g in range(A_GROUPS):
            bmask = jnp.where(row_id == tok, bm_ref[:, g * A_STATE:(g + 1) * A_STATE], 0.0).astype(BF16)
            upd = _mm(xt_ref[g * gw:(g + 1) * gw, :], bmask)
            parts = []
            for h in range(hpg):
                lo = g * gw + h * A_HEAD_DIM
                hn = dec_ref[tok, g * hpg + h] * h_ref[b, lo:lo + A_HEAD_DIM, :] + upd[h * A_HEAD_DIM:(h + 1) * A_HEAD_DIM, :]
                hn_ref[b, lo:lo + A_HEAD_DIM, :] = hn
                parts.append(hn)
            cmask = jnp.where(lane_id == tok, ct_ref[g * A_STATE:(g + 1) * A_STATE, :], 0.0).astype(BF16)
            acc[g] = acc[g] + _mm(jnp.concatenate(parts, axis=0).astype(BF16), cmask)
    for g in range(A_GROUPS):
        yt_ref[g * gw:(g + 1) * gw, :] += acc[g]


def _sample0_state(dec, h0, xdtt, bm, ct):
    n = h0.shape[0]
    tb = SUB
    rows = A_HEADS * A_HEAD_DIM
    blk = pl.BlockSpec((tb, rows, A_STATE), lambda i: (i, 0, 0))
    full = lambda a: pl.BlockSpec(a.shape, lambda i: (0,) * a.ndim)
    return pl.pallas_call(
        _sample0_state_kernel, grid=(n // tb,),
        out_shape=(jax.ShapeDtypeStruct(h0.shape, F32), jax.ShapeDtypeStruct((rows, n), F32)),
        in_specs=[pl.BlockSpec(memory_space=pltpu.SMEM), blk, full(xdtt), full(bm), full(ct)],
        out_specs=(blk, pl.BlockSpec((rows, n), lambda i: (0, 0))),
        compiler_params=pltpu.CompilerParams(dimension_semantics=("arbitrary",), vmem_limit_bytes=VMEM_LIMIT_BYTES),
        name="sample0_state",
    )(dec, h0, xdtt, bm, ct)


def _sample_mid_kernel(yt_ref, xsd_ref, sz_ref, ybg_ref, x_ref, gn_ref, wout0_ref,
                       no_ref, win_ref, lbuf_ref, h0_ref, lcw_ref, lcb_ref,
                       wa_ref, ba_ref, wx_ref, bx_ref, lam_ref,
                       x1_ref, glu_ref, sgc_ref, ydg_ref, nlbuf_ref, hnew_ref):
    ya = _group_rms((yt_ref[...].T + xsd_ref[...]) * sz_ref[...], gn_ref[...])
    mix = jnp.concatenate([ya, ybg_ref[...]], axis=-1).astype(BF16)
    x1 = x_ref[...] + _mm(mix, wout0_ref[...])
    x1_ref[...] = x1
    hn = _rms(x1, no_ref[...]).astype(BF16)
    glu_ref[...] = _mm(hn, win_ref[:, 0:1024]) * jax.nn.sigmoid(_mm(hn, win_ref[:, 1024:2048]))
    sgc_ref[...] = jax.nn.silu(_mm(hn, win_ref[:, 2048:3072]))
    xd = _mm(hn, win_ref[:, 3072:4096])
    acc = lcb_ref[...] + lcw_ref[D_CONV - 1:D_CONV, :] * xd
    for k in range(D_CONV - 1):
        acc = acc + lcw_ref[k:k + 1, :] * lbuf_ref[k]
    for k in range(D_CONV - 2):
        nlbuf_ref[k] = lbuf_ref[k + 1]
    nlbuf_ref[D_CONV - 2] = xd
    a, bt = _lru_gates(acc, wa_ref, ba_ref, wx_ref, bx_ref, lam_ref)
    h = a * h0_ref[...] + bt
    hnew_ref[...] = h
    ydg_ref[...] = h * jax.nn.silu(_mm(hn, win_ref[:, 4096:5120]))


def _sample_mid(yt, xsd, sz, ybg, xs, lbuf, h0, p):
    n = xs.shape[0]
    full = lambda a: pl.BlockSpec(a.shape, lambda i: (0,) * a.ndim)
    args = (yt, xsd, sz, ybg, xs, p["gn"], p["wout0"], p["no"], p["win1"], lbuf, h0, p["lcw"], p["lcb"],
            p["wa"], p["ba"], p["wx"], p["bx"], p["lam"])
    shapes = [(n, 1024), (n, 1024), (n, 1024), (n, 1024), lbuf.shape, (n, 1024)]
    return pl.pallas_call(
        _sample_mid_kernel, grid=(1,),
        out_shape=tuple(jax.ShapeDtypeStruct(s, F32) for s in shapes),
        in_specs=[full(a) for a in args],
        out_specs=tuple(pl.BlockSpec(s, lambda i, nd=len(s): (0,) * nd) for s in shapes),
        compiler_params=pltpu.CompilerParams(dimension_semantics=("arbitrary",), vmem_limit_bytes=VMEM_LIMIT_BYTES),
        name="sample_mid",
    )(*args)


def _sample1_tail_kernel(cbuf_ref, glu_ref, sgc_ref, ydg_ref, x1_ref, ccw_ref, ccb_ref, clg_ref, clb_ref,
                         wout_ref, fn_ref, ncbuf_ref, y_ref):
    hist = C_CONV - 1
    glu = glu_ref[...]
    acc = ccb_ref[...] + ccw_ref[hist:C_CONV, :] * glu
    for k in range(hist):
        acc = acc + ccw_ref[k:k + 1, :] * cbuf_ref[k]
    for k in range(hist - 1):
        ncbuf_ref[k] = cbuf_ref[k + 1]
    ncbuf_ref[hist - 1] = glu
    yc =jax.nn.silu(_ln(acc, clg_ref[...], clb_ref[...])) * sgc_ref[...]
    mix = jnp.concatenate([yc, ydg_ref[...]], axis=-1).astype(BF16)
    out = x1_ref[...] + _mm(mix, wout_ref[...])
    y_ref[...] = _rms(out, fn_ref[...])


def _sample1_tail(cbuf, glu, sgc, ydg, x1, p):
    n = x1.shape[0]
    tb = 16
    row = lambda w: pl.BlockSpec((tb, w), lambda i: (i, 0))
    blk = pl.BlockSpec((cbuf.shape[0], tb, cbuf.shape[2]), lambda i: (0, i, 0))
    const = lambda a: pl.BlockSpec(a.shape, lambda i: (0,) * a.ndim, pipeline_mode=pl.Buffered(1))
    consts = (p["ccw"], p["ccb"], p["clg"], p["clb"], p["wout1"], p["fn"])
    return pl.pallas_call(
        _sample1_tail_kernel, grid=(n // tb,),
        out_shape=(jax.ShapeDtypeStruct(cbuf.shape, F32), jax.ShapeDtypeStruct((n, 1024), F32)),
        in_specs=[blk, row(1024), row(1024), row(1024), row(1024)] + [const(a) for a in consts],
        out_specs=(blk, row(1024)),
        compiler_params=pltpu.CompilerParams(dimension_semantics=("arbitrary",), vmem_limit_bytes=VMEM_LIMIT_BYTES),
        name="sample1_tail",
    )(cbuf, glu, sgc, ydg, x1, *consts)


def _block_diag(w, per_block):
    nh, d, _ = w.shape
    nblk = nh // per_block
    w = w.reshape(nblk, per_block, d, d)
    eye = jnp.eye(per_block, dtype=w.dtype)
    bd = jnp.einsum("nhij,hk->nhikj", w, eye)
    return bd.reshape(nblk, per_block * d, per_block * d).astype(BF16)


def _prepare(norm_even, w_in_even, ssd_conv_w, ssd_conv_b, ssd_dt_bias, ssd_a_log, ssd_d, ssd_norm,
             gmlp_ln_g, gmlp_ln_b, gmlp_w_s, gmlp_b_s, w_out_even,
             norm_odd, w_in_odd, ccv_w, ccv_b, ccv_ln_g, ccv_ln_b,
             lru_conv_w, lru_conv_b, lru_wa, lru_ba, lru_wx, lru_bx, lru_lambda, w_out_odd, final_norm):
    row = lambda v: v.reshape(1, -1).astype(F32)
    w0 = w_in_even[0]
    n_xbc_end = 1024 + A_CONV_DIM
    dtw = jnp.pad(w0[:, n_xbc_end:n_xbc_end + A_HEADS], ((0, 0), (0, LANE - A_HEADS)))
    heads = jnp.arange(LANE)[:, None]
    p = {
        "ne": row(norm_even[0]),
        "win0": jnp.concatenate([w0[:, 0:1024], w0[:, n_xbc_end + A_HEADS:], dtw, w0[:, 1024:n_xbc_end]],
                                axis=-1).astype(BF16),
        "ssd_cw": ssd_conv_w[0], "ssd_cb": row(ssd_conv_b[0]),
        "dtb": jnp.pad(row(ssd_dt_bias[0]), ((0, 0), (0, LANE - A_HEADS))),
        "alog": jnp.pad(row(ssd_a_log[0]), ((0, 0), (0, LANE - A_HEADS))),
        "dsk": row(jnp.repeat(ssd_d[0], A_HEAD_DIM)),
        "gn": row(ssd_norm[0]),
        "lng": row(gmlp_ln_g[0]), "lnb": row(gmlp_ln_b[0]),
        "ws": gmlp_w_s[0],
        "bsf": jnp.repeat(gmlp_b_s[0].T, LANE, axis=1),
        "w00": row(jnp.repeat(gmlp_w_s[0, :, 0, 0], LANE)), "b00": row(jnp.repeat(gmlp_b_s[0, :, 0], LANE)),
        "wout0": w_out_even[0].astype(BF16),
        "ltri": (jnp.arange(CHUNK)[:, None] >= jnp.arange(CHUNK)[None, :]).astype(BF16),
        "e64": (jnp.arange(1024)[None, :] // A_HEAD_DIM == heads).astype(BF16),
        "e128": (jnp.arange(2048)[None, :] // LANE == heads).astype(BF16),
        "no": row(norm_odd[0]),
        "win1": w_in_odd[0].astype(BF16),
        "ccw": ccv_w[0], "ccb": row(ccv_b[0]), "clg": row(ccv_ln_g[0]), "clb": row(ccv_ln_b[0]),
        "lcw": lru_conv_w[0], "lcb": row(lru_conv_b[0]),
        "wa": _block_diag(lru_wa[0], 4), "ba": row(lru_ba[0]),
        "wx": _block_diag(lru_wx[0], 4), "bx": row(lru_bx[0]),
        "lam": row(lru_lambda[0]),
        "wout1": w_out_odd[0].astype(BF16),
        "fn": row(final_norm),
    }
    return p


def kernel(x_prompt, x_sample, state_ssm, state_ssd_conv, state_ccv, state_lru_conv, state_lru, norm_even, w_in_even, ssd_conv_w, ssd_conv_b, ssd_dt_bias, ssd_a_log, ssd_d, ssd_norm, gmlp_ln_g, gmlp_ln_b, gmlp_w_s, gmlp_b_s, w_out_even, norm_odd, w_in_odd, ccv_w, ccv_b, ccv_ln_g, ccv_ln_b, lru_conv_w, lru_conv_b, lru_wa, lru_ba, lru_wx, lru_bx, lru_lambda, w_out_odd, final_norm):
    p = _prepare(norm_even, w_in_even, ssd_conv_w, ssd_conv_b, ssd_dt_bias, ssd_a_log, ssd_d, ssd_norm,
                 gmlp_ln_g, gmlp_ln_b, gmlp_w_s, gmlp_b_s, w_out_even,
                 norm_odd, w_in_odd, ccv_w, ccv_b, ccv_ln_g, ccv_ln_b,
                 lru_conv_w, lru_conv_b, lru_wa, lru_ba, lru_wx, lru_bx, lru_lambda, w_out_odd, final_norm)
    nb = x_prompt.shape[0]
    ns = x_sample.shape[0]

    x1, ssm_p, sbuf_p = _layer0_prompt(x_prompt, p)
    y_prompt, cbuf_p, lbuf_p, lru_p = _layer1_prompt(x1, p)

    xs = x_sample.reshape(ns, D_MODEL)
    tap_major = lambda s: jnp.transpose(s[0], (1, 0, 2))
    seq_major = lambda s: jnp.transpose(s, (1, 0, 2))[None]
    nsbuf, xdtt, dec, bm, ct, xsd, sz, ybg, vn = _sample0_proj(xs, tap_major(state_ssd_conv), p)
    ssm_s, y_ssd = _sample0_state(dec[:, :A_HEADS], state_ssm[0].reshape(ns, A_HEADS * A_HEAD_DIM, A_STATE),
                                  xdtt, bm, ct)
    x1s, glu, sgc, ydg, nlbuf, lru_s = _sample_mid(y_ssd, xsd, sz, ybg, xs, tap_major(state_lru_conv),
                                                   state_lru[0], p)
    ncbuf, y_sample = _sample1_tail(tap_major(state_ccv), glu, sgc, ydg, x1s, p)

    return (y_prompt,
            y_sample.reshape(ns, 1, D_MODEL),
            ssm_p.reshape(1, nb, A_HEADS, A_HEAD_DIM, A_STATE),
            ssm_s.reshape(1, ns, A_HEADS, A_HEAD_DIM, A_STATE),
            sbuf_p[:, SUB - (A_CONV - 1):, :][None],
            seq_major(nsbuf),
            vn.reshape(1, ns, 1, 1024),
            cbuf_p[:, CPAD - (C_CONV - 1):, :][None],
            seq_major(ncbuf),
            lbuf_p[:, SUB - (D_CONV - 1):, :][None],
            seq_major(nlbuf),
            lru_p[:, 0, :][None],
            lru_s[None])
```

```python
import jax
import jax.numpy as jnp
import numpy as np
from jax import lax
from jax.experimental import pallas as pl
from jax.experimental.pallas import tpu as pltpu

F32 = jnp.float32
BF16 = jnp.bfloat16

D_MODEL = 1024
EPS = 1e-6
A_HEADS = 16
A_HEAD_DIM = 64
A_STATE = 128
A_GROUPS = 2
A_CONV = 4
A_CONV_DIM = 1536
CHUNK = 128
B_GROUPS = 8
C_CONV = 31
D_CONV = 4
D_HEADS = 16
D_BLOCK = 64
LRU_C = 8.0

P0_Z, P0_U, P0_V, P0_G, P0_DT, P0_XBC = 0, 1024, 2048, 3072, 4096, 4224
P0_MAIN = 4224
P0_ALL = P0_MAIN + A_CONV_DIM

VMEM_LIMIT_BYTES = 56 * 1024 * 1024

T0 = 512
T1 = 64
LANE = 128
SUB = 8
CPAD = 32


def _mm(a, b):
    return jnp.dot(a, b, preferred_element_type=F32)


def _rms(x, g):
    return x * lax.rsqrt(jnp.mean(x * x, axis=-1, keepdims=True) + EPS) * g


def _ln(x, g, b):
    mu = jnp.mean(x, axis=-1, keepdims=True)
    xc = x - mu
    var = jnp.mean(xc * xc, axis=-1, keepdims=True)
    return xc * lax.rsqrt(var + EPS) * g + b


def _split3(v):
    hi = v.astype(BF16)
    r1 = v - hi.astype(F32)
    mid = r1.astype(BF16)
    lo = (r1 - mid.astype(F32)).astype(BF16)
    return hi, mid, lo


def _mm_exact_rhs(m01, v):
    hi, mid, lo = _split3(v)
    return _mm(m01, hi) + _mm(m01, mid) + _mm(m01, lo)


def _mm_exact_lhs(v, m01):
    hi, mid, lo = _split3(v)
    return _mm(hi, m01) + _mm(mid, m01) + _mm(lo, m01)


def _mm_wide_lhs(v, m01):
    hi = v.astype(BF16)
    mid = (v - hi.astype(F32)).astype(BF16)
    return _mm(hi, m01) + _mm(mid, m01)


def _group_rms(y, gn):
    half = y.shape[-1] // A_GROUPS
    parts = []
    for g in range(A_GROUPS):
        yg = y[:, g * half:(g + 1) * half]
        parts.append(yg * lax.rsqrt(jnp.mean(yg * yg, axis=-1, keepdims=True) + EPS))
    return jnp.concatenate(parts, axis=-1) * gn


def _layer0_prompt_kernel(x_ref, ne_ref, win_ref, cw_ref, cb_ref, dtb_ref, alog_ref, dsk_ref, gn_ref,
                          lng_ref, lnb_ref, ws_ref, bsf_ref, wout_ref, ltri_ref, e64_ref, e128_ref,
                          x1_ref, ssm_ref, cbuf_ref,
                          proj_sc, xpad_sc, xc_sc, mix_sc, state_sc):
    c = pl.program_id(1)
    last = pl.num_programs(1) - 1
    T = x_ref.shape[0]

    @pl.when(c == 0)
    def _():
        state_sc[...] = jnp.zeros_like(state_sc)
        xpad_sc[0:SUB, :] = jnp.zeros((SUB, A_CONV_DIM), F32)

    hn = _rms(x_ref[...], ne_ref[...]).astype(BF16)
    for s in range(0, P0_MAIN, 512):
        e = min(s + 512, P0_MAIN)
        proj_sc[:, s:e] = _mm(hn, win_ref[:, s:e])
    for s in range(0, A_CONV_DIM, 512):
        xpad_sc[SUB:SUB + T, s:s + 512] = _mm(hn, win_ref[:, P0_MAIN + s:P0_MAIN + s + 512])

    acc = cb_ref[...] + cw_ref[A_CONV - 1:A_CONV, :] * xpad_sc[SUB:SUB + T, :]
    for k in range(A_CONV - 1):
        off = SUB - (A_CONV - 1) + k
        acc = acc + cw_ref[k:k + 1, :] * xpad_sc[off:off + T, :]
    xc_sc[...] = jax.nn.silu(acc)
    tail = xpad_sc[T:T + SUB, :]

    @pl.when(c == last)
    def _():
        cbuf_ref[...] = tail

    xpad_sc[0:SUB, :] = tail

    tril = lax.broadcasted_iota(jnp.int32, (CHUNK, CHUNK), 0) >= lax.broadcasted_iota(jnp.int32, (CHUNK, CHUNK), 1)
    lane_lo = lax.broadcasted_iota(jnp.int32, (CHUNK, LANE), 1) < A_HEAD_DIM
    a_neg = -jnp.exp(alog_ref[...])
    gw = A_HEADS // A_GROUPS * A_HEAD_DIM

    for j in range(T // CHUNK):
        rows = slice(j * CHUNK, (j + 1) * CHUNK)
        xs = xc_sc[rows, 0:1024]
        bm = xc_sc[rows, 1024:1024 + A_GROUPS * A_STATE]
        cm = xc_sc[rows, 1024 + A_GROUPS * A_STATE:A_CONV_DIM]
        dt = jax.nn.softplus(proj_sc[rows, P0_DT:P0_DT + LANE] + dtb_ref[...])
        da = dt * a_neg
        cs = _mm_exact_rhs(ltri_ref[...], da)
        cs_t = cs.T
        dt_e = _mm_wide_lhs(dt, e64_ref[...])
        cs_b = _mm_wide_lhs(cs, e128_ref[...])
        cs_e = jnp.concatenate([jnp.where(lane_lo, cs_b[:, 2 * q * LANE:(2 * q + 1) * LANE],
                                          cs_b[:, (2 * q + 1) * LANE:(2 * q + 2) * LANE])
                                for q in range(A_HEADS // 2)], axis=-1)
        xdt = xs * dt_e
        xdt_bf = xdt.astype(BF16)
        xdec_bf = (xdt * jnp.exp(cs_e[CHUNK - 1:CHUNK, :] - cs_e)).astype(BF16)
        grow = jnp.exp(cs_e)

        y_parts = []
        for g in range(A_GROUPS):
            bg = bm[:, g * A_STATE:(g + 1) * A_STATE].astype(BF16)
            cg = cm[:, g * A_STATE:(g + 1) * A_STATE].astype(BF16)
            cb = lax.dot_general(cg, bg, (((1,), (1,)), ((), ())), preferred_element_type=F32)
            st = state_sc[g * gw:(g + 1) * gw, :]
            y_off = lax.dot_general(cg, st.astype(BF16), (((1,), (1,)), ((), ())), preferred_element_type=F32)
            upd = lax.dot_general(xdec_bf[:, g * gw:(g + 1) * gw], bg, (((0,), (0,)), ((), ())),
                                  preferred_element_type=F32)
            for hp in range(A_HEADS // A_GROUPS // 2):
                res = []
                for q in range(2):
                    h = g * (A_HEADS // A_GROUPS) + 2 * hp + q
                    diff = cs_b[:, h * LANE:(h + 1) * LANE] - cs_t[h:h + 1, :]
                    lm = jnp.exp(jnp.where(tril, diff, -jnp.inf))
                    wm = (cb * lm).astype(BF16)
                    res.append(_mm(wm, xdt_bf[:, g * gw + hp * LANE:g * gw + (hp + 1) * LANE]))
                    r_lo = g * gw + (2 * hp + q) * A_HEAD_DIM
                    state_sc[r_lo:r_lo + A_HEAD_DIM, :] = (
                        jnp.exp(cs_t[h:h + 1, CHUNK - 1:CHUNK]) * st[(2 * hp + q) * A_HEAD_DIM:(2 * hp + q + 1) * A_HEAD_DIM, :]
                        + upd[(2 * hp + q) * A_HEAD_DIM:(2 * hp + q + 1) * A_HEAD_DIM, :])
                y_parts.append(jnp.where(lane_lo, res[0], res[1]) + y_off[:, hp * LANE:(hp + 1) * LANE]
                               * grow[:, g * gw + hp * LANE:g * gw + (hp + 1) * LANE])
        y = jnp.concatenate(y_parts, axis=-1) + xs * dsk_ref[...]
        ya = _group_rms(y * jax.nn.silu(proj_sc[rows, P0_Z:P0_Z + 1024]), gn_ref[...])
        mix_sc[rows, 0:1024] = ya.astype(BF16)

        ug = jax.nn.gelu(proj_sc[rows, P0_U:P0_U + 1024])
        vn = _ln(jax.nn.gelu(proj_sc[rows, P0_V:P0_V + 1024]), lng_ref[...], lnb_ref[...])
        vn_bf = vn.astype(BF16)
        mixed = []
        for g in range(B_GROUPS):
            wt = jnp.where(tril, ws_ref[g], 0.0).astype(BF16)
            mixed.append(_mm(wt, vn_bf[:, g * LANE:(g + 1) * LANE]))
        mixed = jnp.concatenate(mixed, axis=-1) + bsf_ref[...]
        yb = ug * mixed * jax.nn.silu(proj_sc[rows, P0_G:P0_G + 1024])
        mix_sc[rows, 1024:2048] = yb.astype(BF16)

    x1_ref[...] = x_ref[...] + _mm(mix_sc[...], wout_ref[...])

    @pl.when(c == last)
    def _():
        ssm_ref[...] = state_sc[...]


def _layer0_prompt(x, p):
    nb, seq, _ = x.shape
    T = T0
    const = lambda shape: pl.BlockSpec(shape, lambda b, c: (0,) * len(shape), pipeline_mode=pl.Buffered(1))
    in_specs = [
        pl.BlockSpec((None, T, D_MODEL), lambda b, c: (b, c, 0)),
        const((1, D_MODEL)), const((D_MODEL, P0_ALL)), const((A_CONV, A_CONV_DIM)), const((1, A_CONV_DIM)),
        const((1, LANE)), const((1, LANE)), const((1, 1024)), const((1, 1024)),
        const((1, 1024)), const((1, 1024)), const((B_GROUPS, CHUNK, CHUNK)), const((CHUNK, 1024)),
        const((2048, D_MODEL)), const((CHUNK, CHUNK)), const((LANE, 1024)), const((LANE, 2048)),
    ]
    out_shape = (jax.ShapeDtypeStruct((nb, seq, D_MODEL), F32),
                 jax.ShapeDtypeStruct((nb, A_HEADS * A_HEAD_DIM, A_STATE), F32),
                 jax.ShapeDtypeStruct((nb, SUB, A_CONV_DIM), F32))
    out_specs = (pl.BlockSpec((None, T, D_MODEL), lambda b, c: (b, c, 0)),
                 pl.BlockSpec((None, A_HEADS * A_HEAD_DIM, A_STATE), lambda b, c: (b, 0, 0)),
                 pl.BlockSpec((None, SUB, A_CONV_DIM), lambda b, c: (b, 0, 0)))
    scratch = [pltpu.VMEM((T, P0_MAIN), F32), pltpu.VMEM((SUB + T, A_CONV_DIM), F32),
               pltpu.VMEM((T, A_CONV_DIM), F32), pltpu.VMEM((T, 2048), BF16),
               pltpu.VMEM((A_HEADS * A_HEAD_DIM, A_STATE), F32)]
    return pl.pallas_call(
        _layer0_prompt_kernel, out_shape=out_shape, grid=(nb, seq // T),
        in_specs=in_specs, out_specs=out_specs, scratch_shapes=scratch,
        compiler_params=pltpu.CompilerParams(dimension_semantics=("arbitrary", "arbitrary"),
                                             vmem_limit_bytes=VMEM_LIMIT_BYTES),
        name="layer0_prompt",
    )(x, p["ne"], p["win0"], p["ssd_cw"], p["ssd_cb"], p["dtb"], p["alog"], p["dsk"], p["gn"],
      p["lng"], p["lnb"], p["ws"], p["bsf"], p["wout0"], p["ltri"], p["e64"], p["e128"])


def _lru_gates(xc, wa_ref, ba_ref, wx_ref, bx_ref, lam_ref):
    xb = xc.astype(BF16)
    nblk = wa_ref.shape[0]
    w = wa_ref.shape[1]
    r = jnp.concatenate([_mm(xb[:, j * w:(j + 1) * w], wa_ref[j]) for j in range(nblk)], axis=-1)
    i = jnp.concatenate([_mm(xb[:, j * w:(j + 1) * w], wx_ref[j]) for j in range(nblk)], axis=-1)
    r = jax.nn.sigmoid(r + ba_ref[...])
    i = jax.nn.sigmoid(i + bx_ref[...])
    log_a = -LRU_C * r * jax.nn.softplus(-lam_ref[...])
    a = jnp.exp(log_a)
    bt = jnp.sqrt(1.0 - a * a) * (i * xc)
    return a, bt


def _dwconv_rows(pad_ref, w_ref, n_taps, first_row, n_rows, row_step):
    cols = []
    blk = 8 * SUB
    for j in range(pad_ref.shape[1] // LANE):
        lanes = slice(j * LANE, (j + 1) * LANE)
        blocks = []
        for r0 in range(0, n_rows, blk):
            acc = None
            for k in range(n_taps):
                lo = first_row + k * row_step + r0
                term = w_ref[k:k + 1, lanes] * pad_ref[lo:lo + blk, lanes]
                acc = term if acc is None else acc + term
            blocks.append(acc)
        cols.append(jnp.concatenate(blocks, axis=0))
    return jnp.concatenate(cols, axis=-1)


def _layer1_prompt_kernel(x_ref, no_ref, win_ref, ccw_ref, ccb_ref, clg_ref, clb_ref, lcw_ref, lcb_ref,
                          wa_ref, ba_ref, wx_ref, bx_ref, lam_ref, wout_ref, fn_ref,
                          y_ref, ccv_ref, lcv_ref, lru_ref,
                          cpad_sc, lpad_sc, h_sc, mix_sc):
    c = pl.program_id(0)
    last = pl.num_programs(0) - 1
    nb, T, _ = x_ref.shape
    rows_all = nb * T
    chist = CPAD * nb
    lhist = SUB * nb

    @pl.when(c == 0)
    def _():
        cpad_sc[0:chist, :] = jnp.zeros((chist, D_MODEL), F32)
        lpad_sc[0:lhist, :] = jnp.zeros((lhist, D_MODEL), F32)
        h_sc[...] = jnp.zeros_like(h_sc)

    x = jnp.concatenate([x_ref[:, t, :] for t in range(T)], axis=0)
    hn = _rms(x, no_ref[...]).astype(BF16)
    cpad_sc[chist:chist + rows_all, :] = (_mm(hn, win_ref[:, 0:1024])
                                          * jax.nn.sigmoid(_mm(hn, win_ref[:, 1024:2048])))
    lpad_sc[lhist:lhist + rows_all, :] = _mm(hn, win_ref[:, 3072:4096])
    sgc = jax.nn.silu(_mm(hn, win_ref[:, 2048:3072]))
    sgd = jax.nn.silu(_mm(hn, win_ref[:, 4096:5120]))

    cacc = _dwconv_rows(cpad_sc, ccw_ref, C_CONV, (CPAD - (C_CONV - 1)) * nb, rows_all, nb) + ccb_ref[...]
    mix_sc[:, 0:1024] = (jax.nn.silu(_ln(cacc, clg_ref[...], clb_ref[...])) * sgc).astype(BF16)

    xc = _dwconv_rows(lpad_sc, lcw_ref, D_CONV, (SUB - (D_CONV - 1)) * nb, rows_all, nb) + lcb_ref[...]
    a, bt = _lru_gates(xc, wa_ref, ba_ref, wx_ref, bx_ref, lam_ref)
    h = h_sc[...]
    hs = []
    for t in range(T):
        h = a[t * nb:(t + 1) * nb, :] * h + bt[t * nb:(t + 1) * nb, :]
        hs.append(h)
    h_sc[...] = h
    mix_sc[:, 1024:2048] = (jnp.concatenate(hs, axis=0) * sgd).astype(BF16)

    y = _rms(x + _mm(mix_sc[...], wout_ref[...]), fn_ref[...])
    for t in range(T):
        y_ref[:, t, :] = y[t * nb:(t + 1) * nb, :]

    @pl.when(c == last)
    def _():
        for r in range(CPAD):
            ccv_ref[:, r, :] = cpad_sc[(T + r) * nb:(T + r + 1) * nb, :]
        for r in range(SUB):
            lcv_ref[:, r, :] = lpad_sc[(T + r) * nb:(T + r + 1) * nb, :]
        lru_ref[...] = h

    cpad_sc[0:chist, :] = cpad_sc[rows_all:rows_all + chist, :]
    lpad_sc[0:lhist, :] = lpad_sc[rows_all:rows_all + lhist, :]


def _layer1_prompt(x1, p):
    nb, seq, _ = x1.shape
    T = T1
    rows_all = nb * T
    const = lambda shape: pl.BlockSpec(shape, lambda c: (0,) * len(shape), pipeline_mode=pl.Buffered(1))
    nblk, wblk = p["wa"].shape[0], p["wa"].shape[1]
    in_specs = [
        pl.BlockSpec((nb, T, D_MODEL), lambda c: (0, c, 0)),
        const((1, D_MODEL)), const((D_MODEL, 5120)), const((C_CONV, 1024)), const((1, 1024)), const((1, 1024)),
        const((1, 1024)), const((D_CONV, 1024)), const((1, 1024)),
        const((nblk, wblk, wblk)), const((1, 1024)), const((nblk, wblk, wblk)), const((1, 1024)), const((1, 1024)),
        const((2048, D_MODEL)), const((1, D_MODEL)),
    ]
    out_shape = (jax.ShapeDtypeStruct((nb, seq, D_MODEL), F32),
                 jax.ShapeDtypeStruct((nb, CPAD, 1024), F32),
                 jax.ShapeDtypeStruct((nb, SUB, 1024), F32),
                 jax.ShapeDtypeStruct((nb, 1024), F32))
    out_specs = (pl.BlockSpec((nb, T, D_MODEL), lambda c: (0, c, 0)),
                 pl.BlockSpec((nb, CPAD, 1024), lambda c: (0, 0, 0)),
                 pl.BlockSpec((nb, SUB, 1024), lambda c: (0, 0, 0)),
                 pl.BlockSpec((nb, 1024), lambda c: (0, 0)))
    scratch = [pltpu.VMEM(((CPAD + T) * nb, 1024), F32), pltpu.VMEM(((SUB + T) * nb, 1024), F32),
               pltpu.VMEM((nb, 1024), F32), pltpu.VMEM((rows_all, 2048), BF16)]
    return pl.pallas_call(
        _layer1_prompt_kernel, out_shape=out_shape, grid=(seq // T,),
        in_specs=in_specs, out_specs=out_specs, scratch_shapes=scratch,
        compiler_params=pltpu.CompilerParams(dimension_semantics=("arbitrary",),
                                             vmem_limit_bytes=VMEM_LIMIT_BYTES),
        name="layer1_prompt",
    )(x1, p["no"], p["win1"], p["ccw"], p["ccb"], p["clg"], p["clb"], p["lcw"], p["lcb"],
      p["wa"], p["ba"], p["wx"], p["bx"], p["lam"], p["wout1"], p["fn"])


def _sample0_proj_kernel(x_ref, ne_ref, win_ref, sbuf_ref, cw_ref, cb_ref, dtb_ref, alog_ref, dsk_ref,
                         lng_ref, lnb_ref, w00_ref, b00_ref, e64_ref,
                         nbuf_ref, xdtt_ref, dec_ref, bm_ref, ct_ref, xsd_ref, sz_ref, ybg_ref, vn_ref):
    hn = _rms(x_ref[...], ne_ref[...]).astype(BF16)
    xbc = _mm(hn, win_ref[:, P0_MAIN:P0_ALL])
    acc = cb_ref[...] + cw_ref[A_CONV - 1:A_CONV, :] * xbc
    for k in range(A_CONV - 1):
        acc = acc + cw_ref[k:k + 1, :] * sbuf_ref[k]
    for k in range(A_CONV - 2):
        nbuf_ref[k] = sbuf_ref[k + 1]
    nbuf_ref[A_CONV - 2] = xbc
    xcv = jax.nn.silu(acc)
    xs = xcv[:, 0:1024]
    nbc = A_GROUPS * A_STATE
    bm_ref[...] = xcv[:, 1024:1024 + nbc]
    ct_ref[...] = xcv[:, 1024 + nbc:A_CONV_DIM].T
    dt = jax.nn.softplus(_mm(hn, win_ref[:, P0_DT:P0_DT + LANE]) + dtb_ref[...])
    dec = jnp.exp(dt * (-jnp.exp(alog_ref[...])))
    xdtt_ref[...] = (xs * _mm_exact_lhs(dt, e64_ref[...])).T.astype(BF16)
    dec_ref[...] = dec
    xsd_ref[...] = xs * dsk_ref[...]
    sz_ref[...] = jax.nn.silu(_mm(hn, win_ref[:, P0_Z:P0_Z + 1024]))
    vn = _ln(jax.nn.gelu(_mm(hn, win_ref[:, P0_V:P0_V + 1024])), lng_ref[...], lnb_ref[...])
    vn_ref[...] = vn
    mixed = w00_ref[...] * vn + b00_ref[...]
    ybg_ref[...] = (jax.nn.gelu(_mm(hn, win_ref[:, P0_U:P0_U + 1024])) * mixed
                    * jax.nn.silu(_mm(hn, win_ref[:, P0_G:P0_G + 1024])))


def _sample0_proj(xs, sbuf, p):
    n = xs.shape[0]
    full = lambda a: pl.BlockSpec(a.shape, lambda i: (0,) * a.ndim)
    args = (xs, p["ne"], p["win0"], sbuf, p["ssd_cw"], p["ssd_cb"], p["dtb"], p["alog"], p["dsk"],
            p["lng"], p["lnb"], p["w00"], p["b00"], p["e64"])
    nbc = A_GROUPS * A_STATE
    shapes = [sbuf.shape, (1024, n), (n, LANE), (n, nbc), (nbc, n), (n, 1024), (n, 1024),
              (n, 1024), (n, 1024)]
    dtypes = [F32, BF16] + [F32] * 7
    return pl.pallas_call(
        _sample0_proj_kernel, grid=(1,),
        out_shape=tuple(jax.ShapeDtypeStruct(s, d) for s, d in zip(shapes, dtypes)),
        in_specs=[full(a) for a in args],
        out_specs=tuple(pl.BlockSpec(s, lambda i, nd=len(s): (0,) * nd) for s in shapes),
        compiler_params=pltpu.CompilerParams(dimension_semantics=("arbitrary",), vmem_limit_bytes=VMEM_LIMIT_BYTES),
        name="sample0_proj",
    )(*args)


def _sample0_state_kernel(dec_ref, h_ref, xt_ref, bm_ref, ct_ref, hn_ref, yt_ref):
    i = pl.program_id(0)
    tb = h_ref.shape[0]
    ntok = bm_ref.shape[0]
    hpg = A_HEADS // A_GROUPS
    gw = hpg * A_HEAD_DIM

    @pl.when(i == 0)
    def _():
        yt_ref[...] = jnp.zeros_like(yt_ref)

    row_id = lax.broadcasted_iota(jnp.int32, (ntok, A_STATE), 0)
    lane_id = lax.broadcasted_iota(jnp.int32, (A_STATE, ntok), 1)
    acc = [jnp.zeros((gw, ntok), F32) for _ in range(A_GROUPS)]
    for b in range(tb):
        tok = i * tb + b
        for g in range(A_GROUPS):
            bmask = jnp.where(row_id == tok, bm_ref[:, g * A_STATE:(g + 1) * A_STATE], 0.0).astype(BF16)
            upd = _mm(xt_ref[g * gw:(g + 1) * gw, :], bmask)
            parts = []
            for h in range(hpg):
                lo = g * gw + h * A_HEAD_DIM
                hn = dec_ref[tok, g * hpg + h] * h_ref[b, lo:lo + A_HEAD_DIM, :] + upd[h * A_HEAD_DIM:(h + 1) * A_HEAD_DIM, :]
                hn_ref[b, lo:lo + A_HEAD_DIM, :] = hn
                parts.append(hn)
            cmask = jnp.where(lane_id == tok, ct_ref[g * A_STATE:(g + 1) * A_STATE, :], 0.0).astype(BF16)
            acc[g] = acc[g] + _mm(jnp.concatenate(parts, axis=0).astype(BF16), cmask)
    for g in range(A_GROUPS):
        yt_ref[g * gw:(g + 1) * gw, :] += acc[g]


def _sample0_state(dec, h0, xdtt, bm, ct):
    n = h0.shape[0]
    tb = SUB
    rows = A_HEADS * A_HEAD_DIM
    blk = pl.BlockSpec((tb, rows, A_STATE), lambda i: (i, 0, 0))
    full = lambda a: pl.BlockSpec(a.shape, lambda i: (0,) * a.ndim)
    return pl.pallas_call(
        _sample0_state_kernel, grid=(n // tb,),
        out_shape=(jax.ShapeDtypeStruct(h0.shape, F32), jax.ShapeDtypeStruct((rows, n), F32)),
        in_specs=[pl.BlockSpec(memory_space=pltpu.SMEM), blk, full(xdtt), full(bm), full(ct)],
        out_specs=(blk, pl.BlockSpec((rows, n), lambda i: (0, 0))),
        compiler_params=pltpu.CompilerParams(dimension_semantics=("arbitrary",), vmem_limit_bytes=VMEM_LIMIT_BYTES),
        name="sample0_state",
    )(dec, h0, xdtt, bm, ct)


def _sample_mid_kernel(yt_ref, xsd_ref, sz_ref, ybg_ref, x_ref, gn_ref, wout0_ref,
                       no_ref, win_ref, lbuf_ref, h0_ref, lcw_ref, lcb_ref,
                       wa_ref, ba_ref, wx_ref, bx_ref, lam_ref,
                       x1_ref, glu_ref, sgc_ref, ydg_ref, nlbuf_ref, hnew_ref):
    ya = _group_rms((yt_ref[...].T + xsd_ref[...]) * sz_ref[...], gn_ref[...])
    mix = jnp.concatenate([ya, ybg_ref[...]], axis=-1).astype(BF16)
    x1 = x_ref[...] + _mm(mix, wout0_ref[...])
    x1_ref[...] = x1
    hn = _rms(x1, no_ref[...]).astype(BF16)
    glu_ref[...] = _mm(hn, win_ref[:, 0:1024]) * jax.nn.sigmoid(_mm(hn, win_ref[:, 1024:2048]))
    sgc_ref[...] = jax.nn.silu(_mm(hn, win_ref[:, 2048:3072]))
    xd = _mm(hn, win_ref[:, 3072:4096])
    acc = lcb_ref[...] + lcw_ref[D_CONV - 1:D_CONV, :] * xd
    for k in range(D_CONV - 1):
        acc = acc + lcw_ref[k:k + 1, :] * lbuf_ref[k]
    for k in range(D_CONV - 2):
        nlbuf_ref[k] = lbuf_ref[k + 1]
    nlbuf_ref[D_CONV - 2] = xd
    a, bt = _lru_gates(acc, wa_ref, ba_ref, wx_ref, bx_ref, lam_ref)
    h = a * h0_ref[...] + bt
    hnew_ref[...] = h
    ydg_ref[...] = h * jax.nn.silu(_mm(hn, win_ref[:, 4096:5120]))


def _sample_mid(yt, xsd, sz, ybg, xs, lbuf, h0, p):
    n = xs.shape[0]
    full = lambda a: pl.BlockSpec(a.shape, lambda i: (0,) * a.ndim)
    args = (yt, xsd, sz, ybg, xs, p["gn"], p["wout0"], p["no"], p["win1"], lbuf, h0, p["lcw"], p["lcb"],
            p["wa"], p["ba"], p["wx"], p["bx"], p["lam"])
    shapes = [(n, 1024), (n, 1024), (n, 1024), (n, 1024), lbuf.shape, (n, 1024)]
    return pl.pallas_call(
        _sample_mid_kernel, grid=(1,),
        out_shape=tuple(jax.ShapeDtypeStruct(s, F32) for s in shapes),
        in_specs=[full(a) for a in args],
        out_specs=tuple(pl.BlockSpec(s, lambda i, nd=len(s): (0,) * nd) for s in shapes),
        compiler_params=pltpu.CompilerParams(dimension_semantics=("arbitrary",), vmem_limit_bytes=VMEM_LIMIT_BYTES),
        name="sample_mid",
    )(*args)


def _sample1_tail_kernel(cbuf_ref, glu_ref, sgc_ref, ydg_ref, x1_ref, ccw_ref, ccb_ref, clg_ref, clb_ref,
                         wout_ref, fn_ref, ncbuf_ref, y_ref):
    hist = C_CONV - 1
    glu = glu_ref[...]
    acc = ccb_ref[...] + ccw_ref[hist:C_CONV, :] * glu
    for k in range(hist):
        acc = acc + ccw_ref[k:k + 1, :] * cbuf_ref[k]
    for k in range(hist - 1):
        ncbuf_ref[k] = cbuf_ref[k + 1]
    ncbuf_ref[hist - 1] = glu
    yc =jax.nn.silu(_ln(acc, clg_ref[...], clb_ref[...])) * sgc_ref[...]
    mix = jnp.concatenate([yc, ydg_ref[...]], axis=-1).astype(BF16)
    out = x1_ref[...] + _mm(mix, wout_ref[...])
    y_ref[...] = _rms(out, fn_ref[...])


def _sample1_tail(cbuf, glu, sgc, ydg, x1, p):
    n = x1.shape[0]
    tb = 32
    row = lambda w: pl.BlockSpec((tb, w), lambda i: (i, 0))
    blk = pl.BlockSpec((cbuf.shape[0], tb, cbuf.shape[2]), lambda i: (0, i, 0))
    const = lambda a: pl.BlockSpec(a.shape, lambda i: (0,) * a.ndim, pipeline_mode=pl.Buffered(1))
    consts = (p["ccw"], p["ccb"], p["clg"], p["clb"], p["wout1"], p["fn"])
    return pl.pallas_call(
        _sample1_tail_kernel, grid=(n // tb,),
        out_shape=(jax.ShapeDtypeStruct(cbuf.shape, F32), jax.ShapeDtypeStruct((n, 1024), F32)),
        in_specs=[blk, row(1024), row(1024), row(1024), row(1024)] + [const(a) for a in consts],
        out_specs=(blk, row(1024)),
        compiler_params=pltpu.CompilerParams(dimension_semantics=("arbitrary",), vmem_limit_bytes=VMEM_LIMIT_BYTES),
        name="sample1_tail",
    )(cbuf, glu, sgc, ydg, x1, *consts)


def _block_diag(w, per_block):
    nh, d, _ = w.shape
    nblk = nh // per_block
    w = w.reshape(nblk, per_block, d, d)
    eye = jnp.eye(per_block, dtype=w.dtype)
    bd = jnp.einsum("nhij,hk->nhikj", w, eye)
    return bd.reshape(nblk, per_block * d, per_block * d).astype(BF16)


def _repack_w_in_even_kernel(w_ref, o_ref):
    n_xbc_end = 1024 + A_CONV_DIM
    rb = w_ref.shape[0]
    o_ref[:, P0_Z:P0_U] = w_ref[:, 0:1024].astype(BF16)
    o_ref[:, P0_U:P0_DT] = w_ref[:, n_xbc_end + A_HEADS:].astype(BF16)
    dtw = jnp.concatenate([w_ref[:, n_xbc_end:n_xbc_end + A_HEADS], jnp.zeros((rb, LANE - A_HEADS), F32)], axis=1)
    o_ref[:, P0_DT:P0_XBC] = dtw.astype(BF16)
    o_ref[:, P0_XBC:P0_ALL] = w_ref[:, 1024:n_xbc_end].astype(BF16)


def _repack_w_in_even(w_in_even):
    _, rows, cols = w_in_even.shape
    rb = LANE
    return pl.pallas_call(
        _repack_w_in_even_kernel, out_shape=jax.ShapeDtypeStruct((rows, P0_ALL), BF16), grid=(rows // rb,),
        in_specs=[pl.BlockSpec((None, rb, cols), lambda i: (0, i, 0))],
        out_specs=pl.BlockSpec((rb, P0_ALL), lambda i: (i, 0)),
        compiler_params=pltpu.CompilerParams(dimension_semantics=("arbitrary",)),
        name="repack_w_in_even",
    )(w_in_even)


def _prepare(norm_even, w_in_even, ssd_conv_w, ssd_conv_b, ssd_dt_bias, ssd_a_log, ssd_d, ssd_norm,
             gmlp_ln_g, gmlp_ln_b, gmlp_w_s, gmlp_b_s, w_out_even,
             norm_odd, w_in_odd, ccv_w, ccv_b, ccv_ln_g, ccv_ln_b,
             lru_conv_w, lru_conv_b, lru_wa, lru_ba, lru_wx, lru_bx, lru_lambda, w_out_odd, final_norm):
    row = lambda v: v.reshape(1, -1).astype(F32)
    heads = np.arange(LANE)[:, None]
    p = {
        "ne": row(norm_even[0]),
        "win0": _repack_w_in_even(w_in_even),
        "ssd_cw": ssd_conv_w[0], "ssd_cb": row(ssd_conv_b[0]),
        "dtb": jnp.pad(row(ssd_dt_bias[0]), ((0, 0), (0, LANE - A_HEADS))),
        "alog": jnp.pad(row(ssd_a_log[0]), ((0, 0), (0, LANE - A_HEADS))),
        "dsk": row(jnp.repeat(ssd_d[0], A_HEAD_DIM)),
        "gn": row(ssd_norm[0]),
        "lng": row(gmlp_ln_g[0]), "lnb": row(gmlp_ln_b[0]),
        "ws": gmlp_w_s[0],
        "bsf": jnp.repeat(gmlp_b_s[0].T, LANE, axis=1),
        "w00": row(jnp.repeat(gmlp_w_s[0, :, 0, 0], LANE)), "b00": row(jnp.repeat(gmlp_b_s[0, :, 0], LANE)),
        "wout0": w_out_even[0].astype(BF16),
        "ltri": jnp.asarray(np.arange(CHUNK)[:, None] >= np.arange(CHUNK)[None, :], BF16),
        "e64": jnp.asarray(np.arange(1024)[None, :] // A_HEAD_DIM == heads, BF16),
        "e128": jnp.asarray(np.arange(2048)[None, :] // LANE == heads, BF16),
        "no": row(norm_odd[0]),
        "win1": w_in_odd[0].astype(BF16),
        "ccw": ccv_w[0], "ccb": row(ccv_b[0]), "clg": row(ccv_ln_g[0]), "clb": row(ccv_ln_b[0]),
        "lcw": lru_conv_w[0], "lcb": row(lru_conv_b[0]),
        "wa": _block_diag(lru_wa[0], 4), "ba": row(lru_ba[0]),
        "wx": _block_diag(lru_wx[0], 4), "bx": row(lru_bx[0]),
        "lam": row(lru_lambda[0]),
        "wout1": w_out_odd[0].astype(BF16),
        "fn": row(final_norm),
    }
    return p


def kernel(x_prompt, x_sample, state_ssm, state_ssd_conv, state_ccv, state_lru_conv, state_lru, norm_even, w_in_even, ssd_conv_w, ssd_conv_b, ssd_dt_bias, ssd_a_log, ssd_d, ssd_norm, gmlp_ln_g, gmlp_ln_b, gmlp_w_s, gmlp_b_s, w_out_even, norm_odd, w_in_odd, ccv_w, ccv_b, ccv_ln_g, ccv_ln_b, lru_conv_w, lru_conv_b, lru_wa, lru_ba, lru_wx, lru_bx, lru_lambda, w_out_odd, final_norm):
    p = _prepare(norm_even, w_in_even, ssd_conv_w, ssd_conv_b, ssd_dt_bias, ssd_a_log, ssd_d, ssd_norm,
                 gmlp_ln_g, gmlp_ln_b, gmlp_w_s, gmlp_b_s, w_out_even,
                 norm_odd, w_in_odd, ccv_w, ccv_b, ccv_ln_g, ccv_ln_b,
                 lru_conv_w, lru_conv_b, lru_wa, lru_ba, lru_wx, lru_bx, lru_lambda, w_out_odd, final_norm)
    nb = x_prompt.shape[0]
    ns = x_sample.shape[0]

    x1, ssm_p, sbuf_p = _layer0_prompt(x_prompt, p)
    y_prompt, cbuf_p, lbuf_p, lru_p = _layer1_prompt(x1, p)

    xs = x_sample.reshape(ns, D_MODEL)
    tap_major = lambda s: jnp.transpose(s[0], (1, 0, 2))
    seq_major = lambda s: jnp.transpose(s, (1, 0, 2))[None]
    nsbuf, xdtt, dec, bm, ct, xsd, sz, ybg, vn = _sample0_proj(xs, tap_major(state_ssd_conv), p)
    ssm_s, y_ssd = _sample0_state(dec[:, :A_HEADS], state_ssm[0].reshape(ns, A_HEADS * A_HEAD_DIM, A_STATE),
                                  xdtt, bm, ct)
    x1s, glu, sgc, ydg, nlbuf, lru_s = _sample_mid(y_ssd, xsd, sz, ybg, xs, tap_major(state_lru_conv),
                                                   state_lru[0], p)
    ncbuf, y_sample = _sample1_tail(tap_major(state_ccv), glu, sgc, ydg, x1s, p)

    return (y_prompt,
            y_sample.reshape(ns, 1, D_MODEL),
            ssm_p.reshape(1, nb, A_HEADS, A_HEAD_DIM, A_STATE),
            ssm_s.reshape(1, ns, A_HEADS, A_HEAD_DIM, A_STATE),
            sbuf_p[:, SUB - (A_CONV - 1):, :][None],
            seq_major(nsbuf),
            vn.reshape(1, ns, 1, 1024),
            cbuf_p[:, CPAD - (C_CONV - 1):, :][None],
            seq_major(ncbuf),
            lbuf_p[:, SUB - (D_CONV - 1):, :][None],
            seq_major(nlbuf),
            lru_p[None],
            lru_s[None])
```

```python
import jax
import jax.numpy as jnp
import numpy as np
from jax import lax
from jax.experimental import pallas as pl
from jax.experimental.pallas import tpu as pltpu

F32 = jnp.float32
BF16 = jnp.bfloat16

D_MODEL = 1024
EPS = 1e-6
A_HEADS = 16
A_HEAD_DIM = 64
A_STATE = 128
A_GROUPS = 2
A_CONV = 4
A_CONV_DIM = 1536
CHUNK = 128
B_GROUPS = 8
C_CONV = 31
D_CONV = 4
D_HEADS = 16
D_BLOCK = 64
LRU_C = 8.0

P0_Z, P0_U, P0_V, P0_G, P0_DT, P0_XBC = 0, 1024, 2048, 3072, 4096, 4224
P0_MAIN = 4224
P0_ALL = P0_MAIN + A_CONV_DIM

VMEM_LIMIT_BYTES = 56 * 1024 * 1024

T0 = 512
T1 = 64
LANE = 128
SUB = 8
CPAD = 32


def _mm(a, b):
    return jnp.dot(a, b, preferred_element_type=F32)


def _mm_nt(a, b):
    return lax.dot_general(a, b, (((1,), (1,)), ((), ())), preferred_element_type=F32)


def _sigmoid(x):
    return 0.5 * jnp.tanh(0.5 * x) + 0.5


def _silu(x):
    return x * _sigmoid(x)


def _rms(x, g):
    return x * lax.rsqrt(jnp.mean(x * x, axis=-1, keepdims=True) + EPS) * g


def _ln(x, g, b):
    mu = jnp.mean(x, axis=-1, keepdims=True)
    xc = x - mu
    var = jnp.mean(xc * xc, axis=-1, keepdims=True)
    return xc * lax.rsqrt(var + EPS) * g + b


def _split3(v):
    hi = v.astype(BF16)
    r1 = v - hi.astype(F32)
    mid = r1.astype(BF16)
    lo = (r1 - mid.astype(F32)).astype(BF16)
    return hi, mid, lo


def _mm_exact_rhs(m01, v):
    hi, mid, lo = _split3(v)
    return _mm(m01, hi) + _mm(m01, mid) + _mm(m01, lo)


def _mm_exact_lhs(v, m01):
    hi, mid, lo = _split3(v)
    return _mm(hi, m01) + _mm(mid, m01) + _mm(lo, m01)


def _mm_wide_lhs(v, m01):
    hi = v.astype(BF16)
    mid = (v - hi.astype(F32)).astype(BF16)
    return _mm(hi, m01) + _mm(mid, m01)


def _group_rms(y, gn):
    half = y.shape[-1] // A_GROUPS
    parts = []
    for g in range(A_GROUPS):
        yg = y[:, g * half:(g + 1) * half]
        parts.append(yg * lax.rsqrt(jnp.mean(yg * yg, axis=-1, keepdims=True) + EPS))
    return jnp.concatenate(parts, axis=-1) * gn


def _layer0_prompt_kernel(x_ref, ne_ref, win_ref, cw_ref, cb_ref, dtb_ref, alog_ref, dsk_ref, gn_ref,
                          lng_ref, lnb_ref, ws_ref, bsf_ref, wout_ref, ltri_ref, e64_ref, e128_ref,
                          x1_ref, ssm_ref, cbuf_ref,
                          proj_sc, xpad_sc, xc_sc, mix_sc, state_sc):
    c = pl.program_id(1)
    last = pl.num_programs(1) - 1
    T = x_ref.shape[0]

    @pl.when(c == 0)
    def _():
        state_sc[...] = jnp.zeros_like(state_sc)
        xpad_sc[0:SUB, :] = jnp.zeros((SUB, A_CONV_DIM), F32)

    hn = _rms(x_ref[...], ne_ref[...]).astype(BF16)
    for s in range(0, P0_MAIN, 512):
        e = min(s + 512, P0_MAIN)
        proj_sc[:, s:e] = _mm_nt(hn, win_ref[s:e, :])
    for s in range(0, A_CONV_DIM, 512):
        xpad_sc[SUB:SUB + T, s:s + 512] = _mm_nt(hn, win_ref[P0_MAIN + s:P0_MAIN + s + 512, :])

    acc = cb_ref[...] + cw_ref[A_CONV - 1:A_CONV, :] * xpad_sc[SUB:SUB + T, :]
    for k in range(A_CONV - 1):
        off = SUB - (A_CONV - 1) + k
        acc = acc + cw_ref[k:k + 1, :] * xpad_sc[off:off + T, :]
    xc_sc[...] = _silu(acc)
    tail = xpad_sc[T:T + SUB, :]

    @pl.when(c == last)
    def _():
        cbuf_ref[...] = tail

    xpad_sc[0:SUB, :] = tail

    tril = lax.broadcasted_iota(jnp.int32, (CHUNK, CHUNK), 0) >= lax.broadcasted_iota(jnp.int32, (CHUNK, CHUNK), 1)
    lane_lo = lax.broadcasted_iota(jnp.int32, (CHUNK, LANE), 1) < A_HEAD_DIM
    a_neg = -jnp.exp(alog_ref[...])
    gw = A_HEADS // A_GROUPS * A_HEAD_DIM

    for j in range(T // CHUNK):
        rows = slice(j * CHUNK, (j + 1) * CHUNK)
        xs = xc_sc[rows, 0:1024]
        bm = xc_sc[rows, 1024:1024 + A_GROUPS * A_STATE]
        cm = xc_sc[rows, 1024 + A_GROUPS * A_STATE:A_CONV_DIM]
        dt = jax.nn.softplus(proj_sc[rows, P0_DT:P0_DT + LANE] + dtb_ref[...])
        da = dt * a_neg
        cs = _mm_exact_rhs(ltri_ref[...], da)
        cs_t = cs.T
        dt_e = _mm_wide_lhs(dt, e64_ref[...])
        cs_b = _mm_wide_lhs(cs, e128_ref[...])
        cs_e = jnp.concatenate([jnp.where(lane_lo, cs_b[:, 2 * q * LANE:(2 * q + 1) * LANE],
                                          cs_b[:, (2 * q + 1) * LANE:(2 * q + 2) * LANE])
                                for q in range(A_HEADS // 2)], axis=-1)
        xdt = xs * dt_e
        xdt_bf = xdt.astype(BF16)
        xdec_bf = (xdt * jnp.exp(cs_e[CHUNK - 1:CHUNK, :] - cs_e)).astype(BF16)
        grow = jnp.exp(cs_e)

        y_parts = []
        for g in range(A_GROUPS):
            bg = bm[:, g * A_STATE:(g + 1) * A_STATE].astype(BF16)
            cg = cm[:, g * A_STATE:(g + 1) * A_STATE].astype(BF16)
            cb = lax.dot_general(cg, bg, (((1,), (1,)), ((), ())), preferred_element_type=F32)
            st = state_sc[g * gw:(g + 1) * gw, :]
            y_off = lax.dot_general(cg, st.astype(BF16), (((1,), (1,)), ((), ())), preferred_element_type=F32)
            upd = lax.dot_general(xdec_bf[:, g * gw:(g + 1) * gw], bg, (((0,), (0,)), ((), ())),
                                  preferred_element_type=F32)
            for hp in range(A_HEADS // A_GROUPS // 2):
                res = []
                for q in range(2):
                    h = g * (A_HEADS // A_GROUPS) + 2 * hp + q
                    diff = cs_b[:, h * LANE:(h + 1) * LANE] - cs_t[h:h + 1, :]
                    lm = jnp.exp(jnp.where(tril, diff, -jnp.inf))
                    wm = (cb * lm).astype(BF16)
                    res.append(_mm(wm, xdt_bf[:, g * gw + hp * LANE:g * gw + (hp + 1) * LANE]))
                    r_lo = g * gw + (2 * hp + q) * A_HEAD_DIM
                    state_sc[r_lo:r_lo + A_HEAD_DIM, :] = (
                        jnp.exp(cs_t[h:h + 1, CHUNK - 1:CHUNK]) * st[(2 * hp + q) * A_HEAD_DIM:(2 * hp + q + 1) * A_HEAD_DIM, :]
                        + upd[(2 * hp + q) * A_HEAD_DIM:(2 * hp + q + 1) * A_HEAD_DIM, :])
                y_parts.append(jnp.where(lane_lo, res[0], res[1]) + y_off[:, hp * LANE:(hp + 1) * LANE]
                               * grow[:, g * gw + hp * LANE:g * gw + (hp + 1) * LANE])
        y = jnp.concatenate(y_parts, axis=-1) + xs * dsk_ref[...]
        ya = _group_rms(y * _silu(proj_sc[rows, P0_Z:P0_Z + 1024]), gn_ref[...])
        mix_sc[rows, 0:1024] = ya.astype(BF16)

        ug = jax.nn.gelu(proj_sc[rows, P0_U:P0_U + 1024])
        vn = _ln(jax.nn.gelu(proj_sc[rows, P0_V:P0_V + 1024]), lng_ref[...], lnb_ref[...])
        vn_bf = vn.astype(BF16)
        mixed = []
        for g in range(B_GROUPS):
            wt = jnp.where(tril, ws_ref[g], 0.0).astype(BF16)
            mixed.append(_mm(wt, vn_bf[:, g * LANE:(g + 1) * LANE]))
        mixed = jnp.concatenate(mixed, axis=-1) + bsf_ref[...]
        yb = ug * mixed * _silu(proj_sc[rows, P0_G:P0_G + 1024])
        mix_sc[rows, 1024:2048] = yb.astype(BF16)

    x1_ref[...] = x_ref[...] + _mm(mix_sc[...], wout_ref[...])

    @pl.when(c == last)
    def _():
        ssm_ref[...] = state_sc[...]


def _layer0_prompt(x, p):
    nb, seq, _ = x.shape
    T = T0
    const = lambda shape: pl.BlockSpec(shape, lambda b, c: (0,) * len(shape), pipeline_mode=pl.Buffered(1))
    in_specs = [
        pl.BlockSpec((None, T, D_MODEL), lambda b, c: (b, c, 0)),
        const((1, D_MODEL)), const((P0_ALL, D_MODEL)), const((A_CONV, A_CONV_DIM)), const((1, A_CONV_DIM)),
        const((1, LANE)), const((1, LANE)), const((1, 1024)), const((1, 1024)),
        const((1, 1024)), const((1, 1024)), const((B_GROUPS, CHUNK, CHUNK)), const((CHUNK, 1024)),
        const((2048, D_MODEL)), const((CHUNK, CHUNK)), const((LANE, 1024)), const((LANE, 2048)),
    ]
    out_shape = (jax.ShapeDtypeStruct((nb, seq, D_MODEL), F32),
                 jax.ShapeDtypeStruct((nb, A_HEADS * A_HEAD_DIM, A_STATE), F32),
                 jax.ShapeDtypeStruct((nb, SUB, A_CONV_DIM), F32))
    out_specs = (pl.BlockSpec((None, T, D_MODEL), lambda b, c: (b, c, 0)),
                 pl.BlockSpec((None, A_HEADS * A_HEAD_DIM, A_STATE), lambda b, c: (b, 0, 0)),
                 pl.BlockSpec((None, SUB, A_CONV_DIM), lambda b, c: (b, 0, 0)))
    scratch = [pltpu.VMEM((T, P0_MAIN), F32), pltpu.VMEM((SUB + T, A_CONV_DIM), F32),
               pltpu.VMEM((T, A_CONV_DIM), F32), pltpu.VMEM((T, 2048), BF16),
               pltpu.VMEM((A_HEADS * A_HEAD_DIM, A_STATE), F32)]
    return pl.pallas_call(
        _layer0_prompt_kernel, out_shape=out_shape, grid=(nb, seq // T),
        in_specs=in_specs, out_specs=out_specs, scratch_shapes=scratch,
        compiler_params=pltpu.CompilerParams(dimension_semantics=("arbitrary", "arbitrary"),
                                             vmem_limit_bytes=VMEM_LIMIT_BYTES),
        name="layer0_prompt",
    )(x, p["ne"], p["win0"], p["ssd_cw"], p["ssd_cb"], p["dtb"], p["alog"], p["dsk"], p["gn"],
      p["lng"], p["lnb"], p["ws"], p["bsf"], p["wout0"], p["ltri"], p["e64"], p["e128"])


def _lru_gates(xc, wa_ref, ba_ref, wx_ref, bx_ref, lam_ref):
    xb = xc.astype(BF16)
    nblk = wa_ref.shape[0]
    w = wa_ref.shape[1]
    r = jnp.concatenate([_mm(xb[:, j * w:(j + 1) * w], wa_ref[j]) for j in range(nblk)], axis=-1)
    i = jnp.concatenate([_mm(xb[:, j * w:(j + 1) * w], wx_ref[j]) for j in range(nblk)], axis=-1)
    r = _sigmoid(r + ba_ref[...])
    i = _sigmoid(i + bx_ref[...])
    log_a = -LRU_C * r * jax.nn.softplus(-lam_ref[...])
    a = jnp.exp(log_a)
    bt = jnp.sqrt(1.0 - a * a) * (i * xc)
    return a, bt


def _dwconv_rows(pad_ref, w_ref, n_taps, first_row, n_rows, row_step):
    cols = []
    blk = 8 * SUB
    for j in range(pad_ref.shape[1] // LANE):
        lanes = slice(j * LANE, (j + 1) * LANE)
        blocks = []
        for r0 in range(0, n_rows, blk):
            acc = None
            for k in range(n_taps):
                lo = first_row + k * row_step + r0
                term = w_ref[k:k + 1, lanes] * pad_ref[lo:lo + blk, lanes]
                acc = term if acc is None else acc + term
            blocks.append(acc)
        cols.append(jnp.concatenate(blocks, axis=0))
    return jnp.concatenate(cols, axis=-1)


def _layer1_prompt_kernel(x_ref, no_ref, win_ref, ccw_ref, ccb_ref, clg_ref, clb_ref, lcw_ref, lcb_ref,
                          wa_ref, ba_ref, wx_ref, bx_ref, lam_ref, wout_ref, fn_ref,
                          y_ref, ccv_ref, lcv_ref, lru_ref,
                          cpad_sc, lpad_sc, h_sc, mix_sc):
    c = pl.program_id(0)
    last = pl.num_programs(0) - 1
    nb, T, _ = x_ref.shape
    rows_all = nb * T
    chist = CPAD * nb
    lhist = SUB * nb

    @pl.when(c == 0)
    def _():
        cpad_sc[0:chist, :] = jnp.zeros((chist, D_MODEL), F32)
        lpad_sc[0:lhist, :] = jnp.zeros((lhist, D_MODEL), F32)
        h_sc[...] = jnp.zeros_like(h_sc)

    x = jnp.concatenate([x_ref[:, t, :] for t in range(T)], axis=0)
    hn = _rms(x, no_ref[...]).astype(BF16)
    cpad_sc[chist:chist + rows_all, :] = (_mm(hn, win_ref[:, 0:1024])
                                          * _sigmoid(_mm(hn, win_ref[:, 1024:2048])))
    lpad_sc[lhist:lhist + rows_all, :] = _mm(hn, win_ref[:, 3072:4096])
    sgc = _silu(_mm(hn, win_ref[:, 2048:3072]))
    sgd = _silu(_mm(hn, win_ref[:, 4096:5120]))

    cacc = _dwconv_rows(cpad_sc, ccw_ref, C_CONV, (CPAD - (C_CONV - 1)) * nb, rows_all, nb) + ccb_ref[...]
    mix_sc[:, 0:1024] = (_silu(_ln(cacc, clg_ref[...], clb_ref[...])) * sgc).astype(BF16)

    xc = _dwconv_rows(lpad_sc, lcw_ref, D_CONV, (SUB - (D_CONV - 1)) * nb, rows_all, nb) + lcb_ref[...]
    a, bt = _lru_gates(xc, wa_ref, ba_ref, wx_ref, bx_ref, lam_ref)
    h = h_sc[...]
    hs = []
    for t in range(T):
        h = a[t * nb:(t + 1) * nb, :] * h + bt[t * nb:(t + 1) * nb, :]
        hs.append(h)
    h_sc[...] = h
    mix_sc[:, 1024:2048] = (jnp.concatenate(hs, axis=0) * sgd).astype(BF16)

    y = _rms(x + _mm(mix_sc[...], wout_ref[...]), fn_ref[...])
    for t in range(T):
        y_ref[:, t, :] = y[t * nb:(t + 1) * nb, :]

    @pl.when(c == last)
    def _():
        for r in range(CPAD):
            ccv_ref[:, r, :] = cpad_sc[(T + r) * nb:(T + r + 1) * nb, :]
        for r in range(SUB):
            lcv_ref[:, r, :] = lpad_sc[(T + r) * nb:(T + r + 1) * nb, :]
        lru_ref[...] = h

    cpad_sc[0:chist, :] = cpad_sc[rows_all:rows_all + chist, :]
    lpad_sc[0:lhist, :] = lpad_sc[rows_all:rows_all + lhist, :]


def _layer1_prompt(x1, p):
    nb, seq, _ = x1.shape
    T = T1
    rows_all = nb * T
    const = lambda shape: pl.BlockSpec(shape, lambda c: (0,) * len(shape), pipeline_mode=pl.Buffered(1))
    nblk, wblk = p["wa"].shape[0], p["wa"].shape[1]
    in_specs = [
        pl.BlockSpec((nb, T, D_MODEL), lambda c: (0, c, 0)),
        const((1, D_MODEL)), const((D_MODEL, 5120)), const((C_CONV, 1024)), const((1, 1024)), const((1, 1024)),
        const((1, 1024)), const((D_CONV, 1024)), const((1, 1024)),
        const((nblk, wblk, wblk)), const((1, 1024)), const((nblk, wblk, wblk)), const((1, 1024)), const((1, 1024)),
        const((2048, D_MODEL)), const((1, D_MODEL)),
    ]
    out_shape = (jax.ShapeDtypeStruct((nb, seq, D_MODEL), F32),
                 jax.ShapeDtypeStruct((nb, CPAD, 1024), F32),
                 jax.ShapeDtypeStruct((nb, SUB, 1024), F32),
                 jax.ShapeDtypeStruct((nb, 1024), F32))
    out_specs = (pl.BlockSpec((nb, T, D_MODEL), lambda c: (0, c, 0)),
                 pl.BlockSpec((nb, CPAD, 1024), lambda c: (0, 0, 0)),
                 pl.BlockSpec((nb, SUB, 1024), lambda c: (0, 0, 0)),
                 pl.BlockSpec((nb, 1024), lambda c: (0, 0)))
    scratch = [pltpu.VMEM(((CPAD + T) * nb, 1024), F32), pltpu.VMEM(((SUB + T) * nb, 1024), F32),
               pltpu.VMEM((nb, 1024), F32), pltpu.VMEM((rows_all, 2048), BF16)]
    return pl.pallas_call(
        _layer1_prompt_kernel, out_shape=out_shape, grid=(seq // T,),
        in_specs=in_specs, out_specs=out_specs, scratch_shapes=scratch,
        compiler_params=pltpu.CompilerParams(dimension_semantics=("arbitrary",),
                                             vmem_limit_bytes=VMEM_LIMIT_BYTES),
        name="layer1_prompt",
    )(x1, p["no"], p["win1"], p["ccw"], p["ccb"], p["clg"], p["clb"], p["lcw"], p["lcb"],
      p["wa"], p["ba"], p["wx"], p["bx"], p["lam"], p["wout1"], p["fn"])


def _sample0_proj_kernel(x_ref, ne_ref, win_ref, sbuf_ref, cw_ref, cb_ref, dtb_ref, alog_ref, dsk_ref,
                         lng_ref, lnb_ref, w00_ref, b00_ref, e64_ref,
                         nbuf_ref, xdtt_ref, dec_ref, bm_ref, ct_ref, xsd_ref, sz_ref, ybg_ref, vn_ref):
    hn = _rms(x_ref[...], ne_ref[...]).astype(BF16)
    xbc = _mm_nt(hn, win_ref[P0_MAIN:P0_ALL, :])
    acc = cb_ref[...] + cw_ref[A_CONV - 1:A_CONV, :] * xbc
    for k in range(A_CONV - 1):
        acc = acc + cw_ref[k:k + 1, :] * sbuf_ref[k]
    for k in range(A_CONV - 2):
        nbuf_ref[k] = sbuf_ref[k + 1]
    nbuf_ref[A_CONV - 2] = xbc
    xcv = _silu(acc)
    xs = xcv[:, 0:1024]
    nbc = A_GROUPS * A_STATE
    bm_ref[...] = xcv[:, 1024:1024 + nbc]
    ct_ref[...] = xcv[:, 1024 + nbc:A_CONV_DIM].T
    dt = jax.nn.softplus(_mm_nt(hn, win_ref[P0_DT:P0_DT + LANE, :]) + dtb_ref[...])
    dec = jnp.exp(dt * (-jnp.exp(alog_ref[...])))
    xdtt_ref[...] = (xs * _mm_exact_lhs(dt, e64_ref[...])).T.astype(BF16)
    dec_ref[...] = dec
    xsd_ref[...] = xs * dsk_ref[...]
    sz_ref[...] = _silu(_mm_nt(hn, win_ref[P0_Z:P0_Z + 1024, :]))
    vn = _ln(jax.nn.gelu(_mm_nt(hn, win_ref[P0_V:P0_V + 1024, :])), lng_ref[...], lnb_ref[...])
    vn_ref[...] = vn
    mixed = w00_ref[...] * vn + b00_ref[...]
    ybg_ref[...] = (jax.nn.gelu(_mm_nt(hn, win_ref[P0_U:P0_U + 1024, :])) * mixed
                    * _silu(_mm_nt(hn, win_ref[P0_G:P0_G + 1024, :])))


def _sample0_proj(xs, sbuf, p):
    n = xs.shape[0]
    full = lambda a: pl.BlockSpec(a.shape, lambda i: (0,) * a.ndim)
    args = (xs, p["ne"], p["win0"], sbuf, p["ssd_cw"], p["ssd_cb"], p["dtb"], p["alog"], p["dsk"],
            p["lng"], p["lnb"], p["w00"], p["b00"], p["e64"])
    nbc = A_GROUPS * A_STATE
    shapes = [sbuf.shape, (1024, n), (n, LANE), (n, nbc), (nbc, n), (n, 1024), (n, 1024),
              (n, 1024), (n, 1024)]
    dtypes = [F32, BF16] + [F32] * 7
    return pl.pallas_call(
        _sample0_proj_kernel, grid=(1,),
        out_shape=tuple(jax.ShapeDtypeStruct(s, d) for s, d in zip(shapes, dtypes)),
        in_specs=[full(a) for a in args],
        out_specs=tuple(pl.BlockSpec(s, lambda i, nd=len(s): (0,) * nd) for s in shapes),
        compiler_params=pltpu.CompilerParams(dimension_semantics=("arbitrary",), vmem_limit_bytes=VMEM_LIMIT_BYTES),
        name="sample0_proj",
    )(*args)


def _sample0_state_kernel(dec_ref, h_ref, xt_ref, bm_ref, ct_ref, hn_ref, yt_ref):
    i = pl.program_id(0)
    tb = h_ref.shape[0]
    ntok = bm_ref.shape[0]
    hpg = A_HEADS // A_GROUPS
    gw = hpg * A_HEAD_DIM

    @pl.when(i == 0)
    def _():
        yt_ref[...] = jnp.zeros_like(yt_ref)

    row_id = lax.broadcasted_iota(jnp.int32, (ntok, A_STATE), 0)
    lane_id = lax.broadcasted_iota(jnp.int32, (A_STATE, ntok), 1)
    acc = [jnp.zeros((gw, ntok), F32) for _ in range(A_GROUPS)]
    for b in range(tb):
        tok = i * tb + b
        for g in range(A_GROUPS):
            bmask = jnp.where(row_id == tok, bm_ref[:, g * A_STATE:(g + 1) * A_STATE], 0.0).astype(BF16)
            upd = _mm(xt_ref[g * gw:(g + 1) * gw, :], bmask)
            parts = []
            for h in range(hpg):
                lo = g * gw + h * A_HEAD_DIM
                hn = dec_ref[tok, g * hpg + h] * h_ref[b, lo:lo + A_HEAD_DIM, :] + upd[h * A_HEAD_DIM:(h + 1) * A_HEAD_DIM, :]
                hn_ref[b, lo:lo + A_HEAD_DIM, :] = hn
                parts.append(hn)
            cmask = jnp.where(lane_id == tok, ct_ref[g * A_STATE:(g + 1) * A_STATE, :], 0.0).astype(BF16)
            acc[g] = acc[g] + _mm(jnp.concatenate(parts, axis=0).astype(BF16), cmask)
    for g in range(A_GROUPS):
        yt_ref[g * gw:(g + 1) * gw, :] += acc[g]


def _sample0_state(dec, h0, xdtt, bm, ct):
    n = h0.shape[0]
    tb = SUB
    rows = A_HEADS * A_HEAD_DIM
    blk = pl.BlockSpec((tb, rows, A_STATE), lambda i: (i, 0, 0))
    full = lambda a: pl.BlockSpec(a.shape, lambda i: (0,) * a.ndim)
    return pl.pallas_call(
        _sample0_state_kernel, grid=(n // tb,),
        out_shape=(jax.ShapeDtypeStruct(h0.shape, F32), jax.ShapeDtypeStruct((rows, n), F32)),
        in_specs=[pl.BlockSpec(memory_space=pltpu.SMEM), blk, full(xdtt), full(bm), full(ct)],
        out_specs=(blk, pl.BlockSpec((rows, n), lambda i: (0, 0))),
        compiler_params=pltpu.CompilerParams(dimension_semantics=("arbitrary",), vmem_limit_bytes=VMEM_LIMIT_BYTES),
        name="sample0_state",
    )(dec, h0, xdtt, bm, ct)


def _sample_mid_kernel(yt_ref, xsd_ref, sz_ref, ybg_ref, x_ref, gn_ref, wout0_ref,
                       no_ref, win_ref, lbuf_ref, h0_ref, lcw_ref, lcb_ref,
                       wa_ref, ba_ref, wx_ref, bx_ref, lam_ref,
                       x1_ref, glu_ref, sgc_ref, ydg_ref, nlbuf_ref, hnew_ref):
    ya = _group_rms((yt_ref[...].T + xsd_ref[...]) * sz_ref[...], gn_ref[...])
    mix = jnp.concatenate([ya, ybg_ref[...]], axis=-1).astype(BF16)
    x1 = x_ref[...] + _mm(mix, wout0_ref[...])
    x1_ref[...] = x1
    hn = _rms(x1, no_ref[...]).astype(BF16)
    glu_ref[...] = _mm(hn, win_ref[:, 0:1024]) * _sigmoid(_mm(hn, win_ref[:, 1024:2048]))
    sgc_ref[...] = _silu(_mm(hn, win_ref[:, 2048:3072]))
    xd = _mm(hn, win_ref[:, 3072:4096])
    acc = lcb_ref[...] + lcw_ref[D_CONV - 1:D_CONV, :] * xd
    for k in range(D_CONV - 1):
        acc = acc + lcw_ref[k:k + 1, :] * lbuf_ref[k]
    for k in range(D_CONV - 2):
        nlbuf_ref[k] = lbuf_ref[k + 1]
    nlbuf_ref[D_CONV - 2] = xd
    a, bt = _lru_gates(acc, wa_ref, ba_ref, wx_ref, bx_ref, lam_ref)
    h = a * h0_ref[...] + bt
    hnew_ref[...] = h
    ydg_ref[...] = h * _silu(_mm(hn, win_ref[:, 4096:5120]))


def _sample_mid(yt, xsd, sz, ybg, xs, lbuf, h0, p):
    n = xs.shape[0]
    full = lambda a: pl.BlockSpec(a.shape, lambda i: (0,) * a.ndim)
    args = (yt, xsd, sz, ybg, xs, p["gn"], p["wout0"], p["no"], p["win1"], lbuf, h0, p["lcw"], p["lcb"],
            p["wa"], p["ba"], p["wx"], p["bx"], p["lam"])
    shapes = [(n, 1024), (n, 1024), (n, 1024), (n, 1024), lbuf.shape, (n, 1024)]
    return pl.pallas_call(
        _sample_mid_kernel, grid=(1,),
        out_shape=tuple(jax.ShapeDtypeStruct(s, F32) for s in shapes),
        in_specs=[full(a) for a in args],
        out_specs=tuple(pl.BlockSpec(s, lambda i, nd=len(s): (0,) * nd) for s in shapes),
        compiler_params=pltpu.CompilerParams(dimension_semantics=("arbitrary",), vmem_limit_bytes=VMEM_LIMIT_BYTES),
        name="sample_mid",
    )(*args)


def _sample1_tail_kernel(cbuf_ref, glu_ref, sgc_ref, ydg_ref, x1_ref, ccw_ref, ccb_ref, clg_ref, clb_ref,
                         wout_ref, fn_ref, ncbuf_ref, y_ref):
    hist = C_CONV - 1
    glu = glu_ref[...]
    acc = ccb_ref[...] + ccw_ref[hist:C_CONV, :] * glu
    for k in range(hist):
        acc = acc + ccw_ref[k:k + 1, :] * cbuf_ref[k]
    for k in range(hist - 1):
        ncbuf_ref[k] = cbuf_ref[k + 1]
    ncbuf_ref[hist - 1] = glu
    yc =_silu(_ln(acc, clg_ref[...], clb_ref[...])) * sgc_ref[...]
    mix = jnp.concatenate([yc, ydg_ref[...]], axis=-1).astype(BF16)
    out = x1_ref[...] + _mm(mix, wout_ref[...])
    y_ref[...] = _rms(out, fn_ref[...])


def _sample1_tail(cbuf, glu, sgc, ydg, x1, p):
    n = x1.shape[0]
    tb = 32
    row = lambda w: pl.BlockSpec((tb, w), lambda i: (i, 0))
    blk = pl.BlockSpec((cbuf.shape[0], tb, cbuf.shape[2]), lambda i: (0, i, 0))
    const = lambda a: pl.BlockSpec(a.shape, lambda i: (0,) * a.ndim, pipeline_mode=pl.Buffered(1))
    consts = (p["ccw"], p["ccb"], p["clg"], p["clb"], p["wout1"], p["fn"])
    return pl.pallas_call(
        _sample1_tail_kernel, grid=(n // tb,),
        out_shape=(jax.ShapeDtypeStruct(cbuf.shape, F32), jax.ShapeDtypeStruct((n, 1024), F32)),
        in_specs=[blk, row(1024), row(1024), row(1024), row(1024)] + [const(a) for a in consts],
        out_specs=(blk, row(1024)),
        compiler_params=pltpu.CompilerParams(dimension_semantics=("arbitrary",), vmem_limit_bytes=VMEM_LIMIT_BYTES),
        name="sample1_tail",
    )(cbuf, glu, sgc, ydg, x1, *consts)


def _block_diag(w, per_block):
    nh, d, _ = w.shape
    nblk = nh // per_block
    w = w.reshape(nblk, per_block, d, d)
    eye = jnp.eye(per_block, dtype=w.dtype)
    bd = jnp.einsum("nhij,hk->nhikj", w, eye)
    return bd.reshape(nblk, per_block * d, per_block * d).astype(BF16)


def _repack_w_in_even_kernel(w_ref, o_ref):
    n_xbc_end = 1024 + A_CONV_DIM
    cols = w_ref.shape[1]
    o_ref[P0_Z:P0_U, :] = w_ref[0:1024, :].astype(BF16)
    o_ref[P0_U:P0_DT, :] = w_ref[n_xbc_end + A_HEADS:, :].astype(BF16)
    o_ref[P0_DT:P0_DT + A_HEADS, :] = w_ref[n_xbc_end:n_xbc_end + A_HEADS, :].astype(BF16)
    o_ref[P0_DT + A_HEADS:P0_XBC, :] = jnp.zeros((LANE - A_HEADS, cols), BF16)
    o_ref[P0_XBC:P0_ALL, :] = w_ref[1024:n_xbc_end, :].astype(BF16)


def _repack_w_in_even(w_t):
    rows, cols = w_t.shape
    cb = 2 * LANE
    return pl.pallas_call(
        _repack_w_in_even_kernel, out_shape=jax.ShapeDtypeStruct((P0_ALL, cols), BF16), grid=(cols // cb,),
        in_specs=[pl.BlockSpec((rows, cb), lambda i: (0, i))],
        out_specs=pl.BlockSpec((P0_ALL, cb), lambda i: (0, i)),
        compiler_params=pltpu.CompilerParams(dimension_semantics=("arbitrary",), vmem_limit_bytes=VMEM_LIMIT_BYTES),
        name="repack_w_in_even",
    )(w_t)


def _prepare(norm_even, w_in_even, ssd_conv_w, ssd_conv_b, ssd_dt_bias, ssd_a_log, ssd_d, ssd_norm,
             gmlp_ln_g, gmlp_ln_b, gmlp_w_s, gmlp_b_s, w_out_even,
             norm_odd, w_in_odd, ccv_w, ccv_b, ccv_ln_g, ccv_ln_b,
             lru_conv_w, lru_conv_b, lru_wa, lru_ba, lru_wx, lru_bx, lru_lambda, w_out_odd, final_norm):
    row = lambda v: v.reshape(1, -1).astype(F32)
    heads = np.arange(LANE)[:, None]
    p = {
        "ne": row(norm_even[0]),
        "win0": _repack_w_in_even(jnp.transpose(w_in_even[0])),
        "ssd_cw": ssd_conv_w[0], "ssd_cb": row(ssd_conv_b[0]),
        "dtb": jnp.pad(row(ssd_dt_bias[0]), ((0, 0), (0, LANE - A_HEADS))),
        "alog": jnp.pad(row(ssd_a_log[0]), ((0, 0), (0, LANE - A_HEADS))),
        "dsk": row(jnp.repeat(ssd_d[0], A_HEAD_DIM)),
        "gn": row(ssd_norm[0]),
        "lng": row(gmlp_ln_g[0]), "lnb": row(gmlp_ln_b[0]),
        "ws": gmlp_w_s[0],
        "bsf": jnp.repeat(gmlp_b_s[0].T, LANE, axis=1),
        "w00": row(jnp.repeat(gmlp_w_s[0, :, 0, 0], LANE)), "b00": row(jnp.repeat(gmlp_b_s[0, :, 0], LANE)),
        "wout0": w_out_even[0].astype(BF16),
        "ltri": jnp.asarray(np.arange(CHUNK)[:, None] >= np.arange(CHUNK)[None, :], BF16),
        "e64": jnp.asarray(np.arange(1024)[None, :] // A_HEAD_DIM == heads, BF16),
        "e128": jnp.asarray(np.arange(2048)[None, :] // LANE == heads, BF16),
        "no": row(norm_odd[0]),
        "win1": w_in_odd[0].astype(BF16),
        "ccw": ccv_w[0], "ccb": row(ccv_b[0]), "clg": row(ccv_ln_g[0]), "clb": row(ccv_ln_b[0]),
        "lcw": lru_conv_w[0], "lcb": row(lru_conv_b[0]),
        "wa": _block_diag(lru_wa[0], 4), "ba": row(lru_ba[0]),
        "wx": _block_diag(lru_wx[0], 4), "bx": row(lru_bx[0]),
        "lam": row(lru_lambda[0]),
        "wout1": w_out_odd[0].astype(BF16),
        "fn": row(final_norm),
    }
    return p


def kernel(x_prompt, x_sample, state_ssm, state_ssd_conv, state_ccv, state_lru_conv, state_lru, norm_even, w_in_even, ssd_conv_w, ssd_conv_b, ssd_dt_bias, ssd_a_log, ssd_d, ssd_norm, gmlp_ln_g, gmlp_ln_b, gmlp_w_s, gmlp_b_s, w_out_even, norm_odd, w_in_odd, ccv_w, ccv_b, ccv_ln_g, ccv_ln_b, lru_conv_w, lru_conv_b, lru_wa, lru_ba, lru_wx, lru_bx, lru_lambda, w_out_odd, final_norm):
    p = _prepare(norm_even, w_in_even, ssd_conv_w, ssd_conv_b, ssd_dt_bias, ssd_a_log, ssd_d, ssd_norm,
                 gmlp_ln_g, gmlp_ln_b, gmlp_w_s, gmlp_b_s, w_out_even,
                 norm_odd, w_in_odd, ccv_w, ccv_b, ccv_ln_g, ccv_ln_b,
                 lru_conv_w, lru_conv_b, lru_wa, lru_ba, lru_wx, lru_bx, lru_lambda, w_out_odd, final_norm)
    nb = x_prompt.shape[0]
    ns = x_sample.shape[0]

    x1, ssm_p, sbuf_p = _layer0_prompt(x_prompt, p)
    y_prompt, cbuf_p, lbuf_p, lru_p = _layer1_prompt(x1, p)

    xs = x_sample.reshape(ns, D_MODEL)
    tap_major = lambda s: jnp.transpose(s[0], (1, 0, 2))
    seq_major = lambda s: jnp.transpose(s, (1, 0, 2))[None]
    nsbuf, xdtt, dec, bm, ct, xsd, sz, ybg, vn = _sample0_proj(xs, tap_major(state_ssd_conv), p)
    ssm_s, y_ssd = _sample0_state(dec[:, :A_HEADS], state_ssm[0].reshape(ns, A_HEADS * A_HEAD_DIM, A_STATE),
                                  xdtt, bm, ct)
    x1s, glu, sgc, ydg, nlbuf, lru_s = _sample_mid(y_ssd, xsd, sz, ybg, xs, tap_major(state_lru_conv),
                                                   state_lru[0], p)
    ncbuf, y_sample = _sample1_tail(tap_major(state_ccv), glu, sgc, ydg, x1s, p)

    return (y_prompt,
            y_sample.reshape(ns, 1, D_MODEL),
            ssm_p.reshape(1, nb, A_HEADS, A_HEAD_DIM, A_STATE),
            ssm_s.reshape(1, ns, A_HEADS, A_HEAD_DIM, A_STATE),
            sbuf_p[:, SUB - (A_CONV - 1):, :][None],
            seq_major(nsbuf),
            vn.reshape(1, ns, 1, 1024),
            cbuf_p[:, CPAD - (C_CONV - 1):, :][None],
            seq_major(ncbuf),
            lbuf_p[:, SUB - (D_CONV - 1):, :][None],
            seq_major(nlbuf),
            lru_p[None],
            lru_s[None])
```

```python
import jax
import jax.numpy as jnp
import numpy as np
from jax import lax
from jax.experimental import pallas as pl
from jax.experimental.pallas import tpu as pltpu

F32 = jnp.float32
BF16 = jnp.bfloat16

D_MODEL = 1024
EPS = 1e-6
A_HEADS = 16
A_HEAD_DIM = 64
A_STATE = 128
A_GROUPS = 2
A_CONV = 4
A_CONV_DIM = 1536
CHUNK = 128
B_GROUPS = 8
C_CONV = 31
D_CONV = 4
D_HEADS = 16
D_BLOCK = 64
LRU_C = 8.0

P0_Z, P0_U, P0_V, P0_G, P0_DT, P0_XBC = 0, 1024, 2048, 3072, 4096, 4224
P0_MAIN = 4224
P0_ALL = P0_MAIN + A_CONV_DIM

VMEM_LIMIT_BYTES = 56 * 1024 * 1024

T0 = 512
T1 = 64
LANE = 128
SUB = 8
CPAD = 32


def _mm(a, b):
    return jnp.dot(a, b, preferred_element_type=F32)


def _mm_nt(a, b):
    return lax.dot_general(a, b, (((1,), (1,)), ((), ())), preferred_element_type=F32)


def _sigmoid(x):
    return 0.5 * jnp.tanh(0.5 * x) + 0.5


def _silu(x):
    return x * _sigmoid(x)


def _rms(x, g):
    return x * lax.rsqrt(jnp.mean(x * x, axis=-1, keepdims=True) + EPS) * g


def _ln(x, g, b):
    mu = jnp.mean(x, axis=-1, keepdims=True)
    xc = x - mu
    var = jnp.mean(xc * xc, axis=-1, keepdims=True)
    return xc * lax.rsqrt(var + EPS) * g + b


def _split3(v):
    hi = v.astype(BF16)
    r1 = v - hi.astype(F32)
    mid = r1.astype(BF16)
    lo = (r1 - mid.astype(F32)).astype(BF16)
    return hi, mid, lo


def _mm_exact_rhs(m01, v):
    hi, mid, lo = _split3(v)
    return _mm(m01, hi) + _mm(m01, mid) + _mm(m01, lo)


def _mm_exact_lhs(v, m01):
    hi, mid, lo = _split3(v)
    return _mm(hi, m01) + _mm(mid, m01) + _mm(lo, m01)


def _mm_wide_lhs(v, m01):
    hi = v.astype(BF16)
    mid = (v - hi.astype(F32)).astype(BF16)
    return _mm(hi, m01) + _mm(mid, m01)


def _group_rms(y, gn):
    half = y.shape[-1] // A_GROUPS
    parts = []
    for g in range(A_GROUPS):
        yg = y[:, g * half:(g + 1) * half]
        parts.append(yg * lax.rsqrt(jnp.mean(yg * yg, axis=-1, keepdims=True) + EPS))
    return jnp.concatenate(parts, axis=-1) * gn


def _layer0_prompt_kernel(x_ref, ne_ref, win_ref, cw_ref, cb_ref, dtb_ref, alog_ref, dsk_ref, gn_ref,
                          lng_ref, lnb_ref, ws_ref, bsf_ref, wout_ref, ltri_ref, e64_ref, e128_ref,
                          x1_ref, ssm_ref, cbuf_ref,
                          proj_sc, xpad_sc, xc_sc, mix_sc, state_sc):
    c = pl.program_id(1)
    last = pl.num_programs(1) - 1
    T = x_ref.shape[0]

    @pl.when(c == 0)
    def _():
        state_sc[...] = jnp.zeros_like(state_sc)
        xpad_sc[0:SUB, :] = jnp.zeros((SUB, A_CONV_DIM), F32)

    hn = _rms(x_ref[...], ne_ref[...]).astype(BF16)
    for s in range(0, P0_MAIN, 512):
        e = min(s + 512, P0_MAIN)
        proj_sc[:, s:e] = _mm(hn, win_ref[:, s:e])
    for s in range(0, A_CONV_DIM, 512):
        xpad_sc[SUB:SUB + T, s:s + 512] = _mm(hn, win_ref[:, P0_MAIN + s:P0_MAIN + s + 512])

    acc = cb_ref[...] + cw_ref[A_CONV - 1:A_CONV, :] * xpad_sc[SUB:SUB + T, :]
    for k in range(A_CONV - 1):
        off = SUB - (A_CONV - 1) + k
        acc = acc + cw_ref[k:k + 1, :] * xpad_sc[off:off + T, :]
    xc_sc[...] = _silu(acc)
    tail = xpad_sc[T:T + SUB, :]

    @pl.when(c == last)
    def _():
        cbuf_ref[...] = tail

    xpad_sc[0:SUB, :] = tail

    tril = lax.broadcasted_iota(jnp.int32, (CHUNK, CHUNK), 0) >= lax.broadcasted_iota(jnp.int32, (CHUNK, CHUNK), 1)
    lane_lo = lax.broadcasted_iota(jnp.int32, (CHUNK, LANE), 1) < A_HEAD_DIM
    a_neg = -jnp.exp(alog_ref[...])
    gw = A_HEADS // A_GROUPS * A_HEAD_DIM

    for j in range(T // CHUNK):
        rows = slice(j * CHUNK, (j + 1) * CHUNK)
        xs = xc_sc[rows, 0:1024]
        bm = xc_sc[rows, 1024:1024 + A_GROUPS * A_STATE]
        cm = xc_sc[rows, 1024 + A_GROUPS * A_STATE:A_CONV_DIM]
        dt = jax.nn.softplus(proj_sc[rows, P0_DT:P0_DT + LANE] + dtb_ref[...])
        da = dt * a_neg
        cs = _mm_exact_rhs(ltri_ref[...], da)
        cs_t = cs.T
        dt_e = _mm_wide_lhs(dt, e64_ref[...])
        cs_b = _mm_wide_lhs(cs, e128_ref[...])
        cs_e = jnp.concatenate([jnp.where(lane_lo, cs_b[:, 2 * q * LANE:(2 * q + 1) * LANE],
                                          cs_b[:, (2 * q + 1) * LANE:(2 * q + 2) * LANE])
                                for q in range(A_HEADS // 2)], axis=-1)
        xdt = xs * dt_e
        xdt_bf = xdt.astype(BF16)
        xdec_bf = (xdt * jnp.exp(cs_e[CHUNK - 1:CHUNK, :] - cs_e)).astype(BF16)
        grow = jnp.exp(cs_e)

        y_parts = []
        for g in range(A_GROUPS):
            bg = bm[:, g * A_STATE:(g + 1) * A_STATE].astype(BF16)
            cg = cm[:, g * A_STATE:(g + 1) * A_STATE].astype(BF16)
            cb = lax.dot_general(cg, bg, (((1,), (1,)), ((), ())), preferred_element_type=F32)
            st = state_sc[g * gw:(g + 1) * gw, :]
            y_off = lax.dot_general(cg, st.astype(BF16), (((1,), (1,)), ((), ())), preferred_element_type=F32)
            upd = lax.dot_general(xdec_bf[:, g * gw:(g + 1) * gw], bg, (((0,), (0,)), ((), ())),
                                  preferred_element_type=F32)
            for hp in range(A_HEADS // A_GROUPS // 2):
                res = []
                for q in range(2):
                    h = g * (A_HEADS // A_GROUPS) + 2 * hp + q
                    diff = cs_b[:, h * LANE:(h + 1) * LANE] - cs_t[h:h + 1, :]
                    lm = jnp.exp(jnp.where(tril, diff, -jnp.inf))
                    wm = (cb * lm).astype(BF16)
                    res.append(_mm(wm, xdt_bf[:, g * gw + hp * LANE:g * gw + (hp + 1) * LANE]))
                    r_lo = g * gw + (2 * hp + q) * A_HEAD_DIM
                    state_sc[r_lo:r_lo + A_HEAD_DIM, :] = (
                        jnp.exp(cs_t[h:h + 1, CHUNK - 1:CHUNK]) * st[(2 * hp + q) * A_HEAD_DIM:(2 * hp + q + 1) * A_HEAD_DIM, :]
                        + upd[(2 * hp + q) * A_HEAD_DIM:(2 * hp + q + 1) * A_HEAD_DIM, :])
                y_parts.append(jnp.where(lane_lo, res[0], res[1]) + y_off[:, hp * LANE:(hp + 1) * LANE]
                               * grow[:, g * gw + hp * LANE:g * gw + (hp + 1) * LANE])
        y = jnp.concatenate(y_parts, axis=-1) + xs * dsk_ref[...]
        ya = _group_rms(y * _silu(proj_sc[rows, P0_Z:P0_Z + 1024]), gn_ref[...])
        mix_sc[rows, 0:1024] = ya.astype(BF16)

        ug = jax.nn.gelu(proj_sc[rows, P0_U:P0_U + 1024])
        vn = _ln(jax.nn.gelu(proj_sc[rows, P0_V:P0_V + 1024]), lng_ref[...], lnb_ref[...])
        vn_bf = vn.astype(BF16)
        mixed = []
        for g in range(B_GROUPS):
            wt = jnp.where(tril, ws_ref[g], 0.0).astype(BF16)
            mixed.append(_mm(wt, vn_bf[:, g * LANE:(g + 1) * LANE]))
        mixed = jnp.concatenate(mixed, axis=-1) + bsf_ref[...]
        yb = ug * mixed * _silu(proj_sc[rows, P0_G:P0_G + 1024])
        mix_sc[rows, 1024:2048] = yb.astype(BF16)

    x1_ref[...] = x_ref[...] + _mm(mix_sc[...], wout_ref[...])

    @pl.when(c == last)
    def _():
        ssm_ref[...] = state_sc[...]


def _layer0_prompt(x, p):
    nb, seq, _ = x.shape
    T = T0
    const = lambda shape: pl.BlockSpec(shape, lambda b, c: (0,) * len(shape), pipeline_mode=pl.Buffered(1))
    in_specs = [
        pl.BlockSpec((None, T, D_MODEL), lambda b, c: (b, c, 0)),
        const((1, D_MODEL)), const((D_MODEL, P0_ALL)), const((A_CONV, A_CONV_DIM)), const((1, A_CONV_DIM)),
        const((1, LANE)), const((1, LANE)), const((1, 1024)), const((1, 1024)),
        const((1, 1024)), const((1, 1024)), const((B_GROUPS, CHUNK, CHUNK)), const((CHUNK, 1024)),
        const((2048, D_MODEL)), const((CHUNK, CHUNK)), const((LANE, 1024)), const((LANE, 2048)),
    ]
    out_shape = (jax.ShapeDtypeStruct((nb, seq, D_MODEL), F32),
                 jax.ShapeDtypeStruct((nb, A_HEADS * A_HEAD_DIM, A_STATE), F32),
                 jax.ShapeDtypeStruct((nb, SUB, A_CONV_DIM), F32))
    out_specs = (pl.BlockSpec((None, T, D_MODEL), lambda b, c: (b, c, 0)),
                 pl.BlockSpec((None, A_HEADS * A_HEAD_DIM, A_STATE), lambda b, c: (b, 0, 0)),
                 pl.BlockSpec((None, SUB, A_CONV_DIM), lambda b, c: (b, 0, 0)))
    scratch = [pltpu.VMEM((T, P0_MAIN), F32), pltpu.VMEM((SUB + T, A_CONV_DIM), F32),
               pltpu.VMEM((T, A_CONV_DIM), F32), pltpu.VMEM((T, 2048), BF16),
               pltpu.VMEM((A_HEADS * A_HEAD_DIM, A_STATE), F32)]
    return pl.pallas_call(
        _layer0_prompt_kernel, out_shape=out_shape, grid=(nb, seq // T),
        in_specs=in_specs, out_specs=out_specs, scratch_shapes=scratch,
        compiler_params=pltpu.CompilerParams(dimension_semantics=("arbitrary", "arbitrary"),
                                             vmem_limit_bytes=VMEM_LIMIT_BYTES),
        name="layer0_prompt",
    )(x, p["ne"], p["win0"], p["ssd_cw"], p["ssd_cb"], p["dtb"], p["alog"], p["dsk"], p["gn"],
      p["lng"], p["lnb"], p["ws"], p["bsf"], p["wout0"], p["ltri"], p["e64"], p["e128"])


def _lru_gates(xc, wa_ref, ba_ref, wx_ref, bx_ref, lam_ref):
    xb = xc.astype(BF16)
    nblk = wa_ref.shape[0]
    w = wa_ref.shape[1]
    r = jnp.concatenate([_mm(xb[:, j * w:(j + 1) * w], wa_ref[j]) for j in range(nblk)], axis=-1)
    i = jnp.concatenate([_mm(xb[:, j * w:(j + 1) * w], wx_ref[j]) for j in range(nblk)], axis=-1)
    r = _sigmoid(r + ba_ref[...])
    i = _sigmoid(i + bx_ref[...])
    log_a = -LRU_C * r * jax.nn.softplus(-lam_ref[...])
    a = jnp.exp(log_a)
    bt = jnp.sqrt(1.0 - a * a) * (i * xc)
    return a, bt


def _dwconv_rows(pad_ref, w_ref, n_taps, first_row, n_rows, row_step):
    cols = []
    blk = 8 * SUB
    for j in range(pad_ref.shape[1] // LANE):
        lanes = slice(j * LANE, (j + 1) * LANE)
        blocks = []
        for r0 in range(0, n_rows, blk):
            acc = None
            for k in range(n_taps):
                lo = first_row + k * row_step + r0
                term = w_ref[k:k + 1, lanes] * pad_ref[lo:lo + blk, lanes]
                acc = term if acc is None else acc + term
            blocks.append(acc)
        cols.append(jnp.concatenate(blocks, axis=0))
    return jnp.concatenate(cols, axis=-1)


def _layer1_prompt_kernel(x_ref, no_ref, win_ref, ccw_ref, ccb_ref, clg_ref, clb_ref, lcw_ref, lcb_ref,
                          wa_ref, ba_ref, wx_ref, bx_ref, lam_ref, wout_ref, fn_ref,
                          y_ref, ccv_ref, lcv_ref, lru_ref,
                          cpad_sc, lpad_sc, h_sc, mix_sc):
    c = pl.program_id(0)
    last = pl.num_programs(0) - 1
    nb, T, _ = x_ref.shape
    rows_all = nb * T
    chist = CPAD * nb
    lhist = SUB * nb

    @pl.when(c == 0)
    def _():
        cpad_sc[0:chist, :] = jnp.zeros((chist, D_MODEL), F32)
        lpad_sc[0:lhist, :] = jnp.zeros((lhist, D_MODEL), F32)
        h_sc[...] = jnp.zeros_like(h_sc)

    x = jnp.concatenate([x_ref[:, t, :] for t in range(T)], axis=0)
    hn = _rms(x, no_ref[...]).astype(BF16)
    cpad_sc[chist:chist + rows_all, :] = (_mm(hn, win_ref[:, 0:1024])
                                          * _sigmoid(_mm(hn, win_ref[:, 1024:2048])))
    lpad_sc[lhist:lhist + rows_all, :] = _mm(hn, win_ref[:, 3072:4096])
    sgc = _silu(_mm(hn, win_ref[:, 2048:3072]))
    sgd = _silu(_mm(hn, win_ref[:, 4096:5120]))

    cacc = _dwconv_rows(cpad_sc, ccw_ref, C_CONV, (CPAD - (C_CONV - 1)) * nb, rows_all, nb) + ccb_ref[...]
    mix_sc[:, 0:1024] = (_silu(_ln(cacc, clg_ref[...], clb_ref[...])) * sgc).astype(BF16)

    xc = _dwconv_rows(lpad_sc, lcw_ref, D_CONV, (SUB - (D_CONV - 1)) * nb, rows_all, nb) + lcb_ref[...]
    a, bt = _lru_gates(xc, wa_ref, ba_ref, wx_ref, bx_ref, lam_ref)
    h = h_sc[...]
    hs = []
    for t in range(T):
        h = a[t * nb:(t + 1) * nb, :] * h + bt[t * nb:(t + 1) * nb, :]
        hs.append(h)
    h_sc[...] = h
    mix_sc[:, 1024:2048] = (jnp.concatenate(hs, axis=0) * sgd).astype(BF16)

    y = _rms(x + _mm(mix_sc[...], wout_ref[...]), fn_ref[...])
    for t in range(T):
        y_ref[:, t, :] = y[t * nb:(t + 1) * nb, :]

    @pl.when(c == last)
    def _():
        for r in range(CPAD):
            ccv_ref[:, r, :] = cpad_sc[(T + r) * nb:(T + r + 1) * nb, :]
        for r in range(SUB):
            lcv_ref[:, r, :] = lpad_sc[(T + r) * nb:(T + r + 1) * nb, :]
        lru_ref[...] = h

    cpad_sc[0:chist, :] = cpad_sc[rows_all:rows_all + chist, :]
    lpad_sc[0:lhist, :] = lpad_sc[rows_all:rows_all + lhist, :]


def _layer1_prompt(x1, p):
    nb, seq, _ = x1.shape
    T = T1
    rows_all = nb * T
    const = lambda shape: pl.BlockSpec(shape, lambda c: (0,) * len(shape), pipeline_mode=pl.Buffered(1))
    nblk, wblk = p["wa"].shape[0], p["wa"].shape[1]
    in_specs = [
        pl.BlockSpec((nb, T, D_MODEL), lambda c: (0, c, 0)),
        const((1, D_MODEL)), const((D_MODEL, 5120)), const((C_CONV, 1024)), const((1, 1024)), const((1, 1024)),
        const((1, 1024)), const((D_CONV, 1024)), const((1, 1024)),
        const((nblk, wblk, wblk)), const((1, 1024)), const((nblk, wblk, wblk)), const((1, 1024)), const((1, 1024)),
        const((2048, D_MODEL)), const((1, D_MODEL)),
    ]
    out_shape = (jax.ShapeDtypeStruct((nb, seq, D_MODEL), F32),
                 jax.ShapeDtypeStruct((nb, CPAD, 1024), F32),
                 jax.ShapeDtypeStruct((nb, SUB, 1024), F32),
                 jax.ShapeDtypeStruct((nb, 1024), F32))
    out_specs = (pl.BlockSpec((nb, T, D_MODEL), lambda c: (0, c, 0)),
                 pl.BlockSpec((nb, CPAD, 1024), lambda c: (0, 0, 0)),
                 pl.BlockSpec((nb, SUB, 1024), lambda c: (0, 0, 0)),
                 pl.BlockSpec((nb, 1024), lambda c: (0, 0)))
    scratch = [pltpu.VMEM(((CPAD + T) * nb, 1024), F32), pltpu.VMEM(((SUB + T) * nb, 1024), F32),
               pltpu.VMEM((nb, 1024), F32), pltpu.VMEM((rows_all, 2048), BF16)]
    return pl.pallas_call(
        _layer1_prompt_kernel, out_shape=out_shape, grid=(seq // T,),
        in_specs=in_specs, out_specs=out_specs, scratch_shapes=scratch,
        compiler_params=pltpu.CompilerParams(dimension_semantics=("arbitrary",),
                                             vmem_limit_bytes=VMEM_LIMIT_BYTES),
        name="layer1_prompt",
    )(x1, p["no"], p["win1"], p["ccw"], p["ccb"], p["clg"], p["clb"], p["lcw"], p["lcb"],
      p["wa"], p["ba"], p["wx"], p["bx"], p["lam"], p["wout1"], p["fn"])


def _sample0_proj_kernel(x_ref, ne_ref, win_ref, sbuf_ref, cw_ref, cb_ref, dtb_ref, alog_ref, dsk_ref,
                         lng_ref, lnb_ref, w00_ref, b00_ref, e64_ref,
                         nbuf_ref, xdtt_ref, dec_ref, bm_ref, ct_ref, xsd_ref, sz_ref, ybg_ref, vn_ref):
    hn = _rms(x_ref[...], ne_ref[...]).astype(BF16)
    xbc = _mm(hn, win_ref[:, P0_MAIN:P0_ALL])
    acc = cb_ref[...] + cw_ref[A_CONV - 1:A_CONV, :] * xbc
    for k in range(A_CONV - 1):
        acc = acc + cw_ref[k:k + 1, :] * sbuf_ref[k]
    for k in range(A_CONV - 2):
        nbuf_ref[k] = sbuf_ref[k + 1]
    nbuf_ref[A_CONV - 2] = xbc
    xcv = _silu(acc)
    xs = xcv[:, 0:1024]
    nbc = A_GROUPS * A_STATE
    bm_ref[...] = xcv[:, 1024:1024 + nbc]
    ct_ref[...] = xcv[:, 1024 + nbc:A_CONV_DIM].T
    dt = jax.nn.softplus(_mm(hn, win_ref[:, P0_DT:P0_DT + LANE]) + dtb_ref[...])
    dec = jnp.exp(dt * (-jnp.exp(alog_ref[...])))
    xdtt_ref[...] = (xs * _mm_exact_lhs(dt, e64_ref[...])).T.astype(BF16)
    dec_ref[...] = dec
    xsd_ref[...] = xs * dsk_ref[...]
    sz_ref[...] = _silu(_mm(hn, win_ref[:, P0_Z:P0_Z + 1024]))
    vn = _ln(jax.nn.gelu(_mm(hn, win_ref[:, P0_V:P0_V + 1024])), lng_ref[...], lnb_ref[...])
    vn_ref[...] = vn
    mixed = w00_ref[...] * vn + b00_ref[...]
    ybg_ref[...] = (jax.nn.gelu(_mm(hn, win_ref[:, P0_U:P0_U + 1024])) * mixed
                    * _silu(_mm(hn, win_ref[:, P0_G:P0_G + 1024])))


def _sample0_proj(xs, sbuf, p):
    n = xs.shape[0]
    full = lambda a: pl.BlockSpec(a.shape, lambda i: (0,) * a.ndim)
    args = (xs, p["ne"], p["win0"], sbuf, p["ssd_cw"], p["ssd_cb"], p["dtb"], p["alog"], p["dsk"],
            p["lng"], p["lnb"], p["w00"], p["b00"], p["e64"])
    nbc = A_GROUPS * A_STATE
    shapes = [sbuf.shape, (1024, n), (n, LANE), (n, nbc), (nbc, n), (n, 1024), (n, 1024),
              (n, 1024), (n, 1024)]
    dtypes = [F32, BF16] + [F32] * 7
    return pl.pallas_call(
        _sample0_proj_kernel, grid=(1,),
        out_shape=tuple(jax.ShapeDtypeStruct(s, d) for s, d in zip(shapes, dtypes)),
        in_specs=[full(a) for a in args],
        out_specs=tuple(pl.BlockSpec(s, lambda i, nd=len(s): (0,) * nd) for s in shapes),
        compiler_params=pltpu.CompilerParams(dimension_semantics=("arbitrary",), vmem_limit_bytes=VMEM_LIMIT_BYTES),
        name="sample0_proj",
    )(*args)


def _sample0_state_kernel(dec_ref, h_ref, xt_ref, bm_ref, ct_ref, hn_ref, yt_ref):
    i = pl.program_id(0)
    tb = h_ref.shape[0]
    ntok = bm_ref.shape[0]
    hpg = A_HEADS // A_GROUPS
    gw = hpg * A_HEAD_DIM

    @pl.when(i == 0)
    def _():
        yt_ref[...] = jnp.zeros_like(yt_ref)

    row_id = lax.broadcasted_iota(jnp.int32, (ntok, A_STATE), 0)
    lane_id = lax.broadcasted_iota(jnp.int32, (A_STATE, ntok), 1)
    acc = [jnp.zeros((gw, ntok), F32) for _ in range(A_GROUPS)]
    for b in range(tb):
        tok = i * tb + b
        for g in range(A_GROUPS):
            bmask = jnp.where(row_id == tok, bm_ref[:, g * A_STATE:(g + 1) * A_STATE], 0.0).astype(BF16)
            upd = _mm(xt_ref[g * gw:(g + 1) * gw, :], bmask)
            parts = []
            for h in range(hpg):
                lo = g * gw + h * A_HEAD_DIM
                hn = dec_ref[tok, g * hpg + h] * h_ref[b, lo:lo + A_HEAD_DIM, :] + upd[h * A_HEAD_DIM:(h + 1) * A_HEAD_DIM, :]
                hn_ref[b, lo:lo + A_HEAD_DIM, :] = hn
                parts.append(hn)
            cmask = jnp.where(lane_id == tok, ct_ref[g * A_STATE:(g + 1) * A_STATE, :], 0.0).astype(BF16)
            acc[g] = acc[g] + _mm(jnp.concatenate(parts, axis=0).astype(BF16), cmask)
    for g in range(A_GROUPS):
        yt_ref[g * gw:(g + 1) * gw, :] += acc[g]


def _sample0_state(dec, h0, xdtt, bm, ct):
    n = h0.shape[0]
    tb = SUB
    rows = A_HEADS * A_HEAD_DIM
    blk = pl.BlockSpec((tb, rows, A_STATE), lambda i: (i, 0, 0))
    full = lambda a: pl.BlockSpec(a.shape, lambda i: (0,) * a.ndim)
    return pl.pallas_call(
        _sample0_state_kernel, grid=(n // tb,),
        out_shape=(jax.ShapeDtypeStruct(h0.shape, F32), jax.ShapeDtypeStruct((rows, n), F32)),
        in_specs=[pl.BlockSpec(memory_space=pltpu.SMEM), blk, full(xdtt), full(bm), full(ct)],
        out_specs=(blk, pl.BlockSpec((rows, n), lambda i: (0, 0))),
        compiler_params=pltpu.CompilerParams(dimension_semantics=("arbitrary",), vmem_limit_bytes=VMEM_LIMIT_BYTES),
        name="sample0_state",
    )(dec, h0, xdtt, bm, ct)


def _sample_mid_kernel(yt_ref, xsd_ref, sz_ref, ybg_ref, x_ref, gn_ref, wout0_ref,
                       no_ref, win_ref, lbuf_ref, h0_ref, lcw_ref, lcb_ref,
                       wa_ref, ba_ref, wx_ref, bx_ref, lam_ref,
                       x1_ref, glu_ref, sgc_ref, ydg_ref, nlbuf_ref, hnew_ref):
    ya = _group_rms((yt_ref[...].T + xsd_ref[...]) * sz_ref[...], gn_ref[...])
    mix = jnp.concatenate([ya, ybg_ref[...]], axis=-1).astype(BF16)
    x1 = x_ref[...] + _mm(mix, wout0_ref[...])
    x1_ref[...] = x1
    hn = _rms(x1, no_ref[...]).astype(BF16)
    glu_ref[...] = _mm(hn, win_ref[:, 0:1024]) * _sigmoid(_mm(hn, win_ref[:, 1024:2048]))
    sgc_ref[...] = _silu(_mm(hn, win_ref[:, 2048:3072]))
    xd = _mm(hn, win_ref[:, 3072:4096])
    acc = lcb_ref[...] + lcw_ref[D_CONV - 1:D_CONV, :] * xd
    for k in range(D_CONV - 1):
        acc = acc + lcw_ref[k:k + 1, :] * lbuf_ref[k]
    for k in range(D_CONV - 2):
        nlbuf_ref[k] = lbuf_ref[k + 1]
    nlbuf_ref[D_CONV - 2] = xd
    a, bt = _lru_gates(acc, wa_ref, ba_ref, wx_ref, bx_ref, lam_ref)
    h = a * h0_ref[...] + bt
    hnew_ref[...] = h
    ydg_ref[...] = h * _silu(_mm(hn, win_ref[:, 4096:5120]))


def _sample_mid(yt, xsd, sz, ybg, xs, lbuf, h0, p):
    n = xs.shape[0]
    full = lambda a: pl.BlockSpec(a.shape, lambda i: (0,) * a.ndim)
    args = (yt, xsd, sz, ybg, xs, p["gn"], p["wout0"], p["no"], p["win1"], lbuf, h0, p["lcw"], p["lcb"],
            p["wa"], p["ba"], p["wx"], p["bx"], p["lam"])
    shapes = [(n, 1024), (n, 1024), (n, 1024), (n, 1024), lbuf.shape, (n, 1024)]
    return pl.pallas_call(
        _sample_mid_kernel, grid=(1,),
        out_shape=tuple(jax.ShapeDtypeStruct(s, F32) for s in shapes),
        in_specs=[full(a) for a in args],
        out_specs=tuple(pl.BlockSpec(s, lambda i, nd=len(s): (0,) * nd) for s in shapes),
        compiler_params=pltpu.CompilerParams(dimension_semantics=("arbitrary",), vmem_limit_bytes=VMEM_LIMIT_BYTES),
        name="sample_mid",
    )(*args)


def _sample1_tail_kernel(cbuf_ref, glu_ref, sgc_ref, ydg_ref, x1_ref, ccw_ref, ccb_ref, clg_ref, clb_ref,
                         wout_ref, fn_ref, ncbuf_ref, y_ref):
    hist = C_CONV - 1
    glu = glu_ref[...]
    acc = ccb_ref[...] + ccw_ref[hist:C_CONV, :] * glu
    for k in range(hist):
        acc = acc + ccw_ref[k:k + 1, :] * cbuf_ref[k]
    for k in range(hist - 1):
        ncbuf_ref[k] = cbuf_ref[k + 1]
    ncbuf_ref[hist - 1] = glu
    yc =_silu(_ln(acc, clg_ref[...], clb_ref[...])) * sgc_ref[...]
    mix = jnp.concatenate([yc, ydg_ref[...]], axis=-1).astype(BF16)
    out = x1_ref[...] + _mm(mix, wout_ref[...])
    y_ref[...] = _rms(out, fn_ref[...])


def _sample1_tail(cbuf, glu, sgc, ydg, x1, p):
    n = x1.shape[0]
    tb = 32
    row = lambda w: pl.BlockSpec((tb, w), lambda i: (i, 0))
    blk = pl.BlockSpec((cbuf.shape[0], tb, cbuf.shape[2]), lambda i: (0, i, 0))
    const = lambda a: pl.BlockSpec(a.shape, lambda i: (0,) * a.ndim, pipeline_mode=pl.Buffered(1))
    consts = (p["ccw"], p["ccb"], p["clg"], p["clb"], p["wout1"], p["fn"])
    return pl.pallas_call(
        _sample1_tail_kernel, grid=(n // tb,),
        out_shape=(jax.ShapeDtypeStruct(cbuf.shape, F32), jax.ShapeDtypeStruct((n, 1024), F32)),
        in_specs=[blk, row(1024), row(1024), row(1024), row(1024)] + [const(a) for a in consts],
        out_specs=(blk, row(1024)),
        compiler_params=pltpu.CompilerParams(dimension_semantics=("arbitrary",), vmem_limit_bytes=VMEM_LIMIT_BYTES),
        name="sample1_tail",
    )(cbuf, glu, sgc, ydg, x1, *consts)


def _block_diag(w, per_block):
    nh, d, _ = w.shape
    nblk = nh // per_block
    w = w.reshape(nblk, per_block, d, d)
    eye = jnp.eye(per_block, dtype=w.dtype)
    bd = jnp.einsum("nhij,hk->nhikj", w, eye)
    return bd.reshape(nblk, per_block * d, per_block * d).astype(BF16)


def _repack_w_in_even_kernel(w_ref, o_ref):
    n_xbc_end = 1024 + A_CONV_DIM
    cb = w_ref.shape[1]
    eye = (lax.broadcasted_iota(jnp.int32, (cb, cb), 0) == lax.broadcasted_iota(jnp.int32, (cb, cb), 1)).astype(BF16)
    tr = lambda piece: _mm_nt(eye, piece.astype(BF16)).astype(BF16)
    o_ref[:, P0_Z:P0_U] = tr(w_ref[0:1024, :])
    o_ref[:, P0_U:P0_DT] = tr(w_ref[n_xbc_end + A_HEADS:, :])
    o_ref[:, P0_DT:P0_XBC] = tr(jnp.concatenate([w_ref[n_xbc_end:n_xbc_end + A_HEADS, :],
                                                 jnp.zeros((LANE - A_HEADS, cb), F32)], axis=0))
    o_ref[:, P0_XBC:P0_ALL] = tr(w_ref[1024:n_xbc_end, :])


def _repack_w_in_even(w_t):
    rows, cols = w_t.shape
    cb = 2 * LANE
    return pl.pallas_call(
        _repack_w_in_even_kernel, out_shape=jax.ShapeDtypeStruct((cols, P0_ALL), BF16), grid=(cols // cb,),
        in_specs=[pl.BlockSpec((rows, cb), lambda i: (0, i))],
        out_specs=pl.BlockSpec((cb, P0_ALL), lambda i: (i, 0)),
        compiler_params=pltpu.CompilerParams(dimension_semantics=("arbitrary",), vmem_limit_bytes=VMEM_LIMIT_BYTES),
        name="repack_w_in_even",
    )(w_t)


def _prepare(norm_even, w_in_even, ssd_conv_w, ssd_conv_b, ssd_dt_bias, ssd_a_log, ssd_d, ssd_norm,
             gmlp_ln_g, gmlp_ln_b, gmlp_w_s, gmlp_b_s, w_out_even,
             norm_odd, w_in_odd, ccv_w, ccv_b, ccv_ln_g, ccv_ln_b,
             lru_conv_w, lru_conv_b, lru_wa, lru_ba, lru_wx, lru_bx, lru_lambda, w_out_odd, final_norm):
    row = lambda v: v.reshape(1, -1).astype(F32)
    heads = np.arange(LANE)[:, None]
    p = {
        "ne": row(norm_even[0]),
        "win0": _repack_w_in_even(jnp.transpose(w_in_even[0])),
        "ssd_cw": ssd_conv_w[0], "ssd_cb": row(ssd_conv_b[0]),
        "dtb": jnp.pad(row(ssd_dt_bias[0]), ((0, 0), (0, LANE - A_HEADS))),
        "alog": jnp.pad(row(ssd_a_log[0]), ((0, 0), (0, LANE - A_HEADS))),
        "dsk": row(jnp.repeat(ssd_d[0], A_HEAD_DIM)),
        "gn": row(ssd_norm[0]),
        "lng": row(gmlp_ln_g[0]), "lnb": row(gmlp_ln_b[0]),
        "ws": gmlp_w_s[0],
        "bsf": jnp.repeat(gmlp_b_s[0].T, LANE, axis=1),
        "w00": row(jnp.repeat(gmlp_w_s[0, :, 0, 0], LANE)), "b00": row(jnp.repeat(gmlp_b_s[0, :, 0], LANE)),
        "wout0": w_out_even[0].astype(BF16),
        "ltri": jnp.asarray(np.arange(CHUNK)[:, None] >= np.arange(CHUNK)[None, :], BF16),
        "e64": jnp.asarray(np.arange(1024)[None, :] // A_HEAD_DIM == heads, BF16),
        "e128": jnp.asarray(np.arange(2048)[None, :] // LANE == heads, BF16),
        "no": row(norm_odd[0]),
        "win1": w_in_odd[0].astype(BF16),
        "ccw": ccv_w[0], "ccb": row(ccv_b[0]), "clg": row(ccv_ln_g[0]), "clb": row(ccv_ln_b[0]),
        "lcw": lru_conv_w[0], "lcb": row(lru_conv_b[0]),
        "wa": _block_diag(lru_wa[0], 4), "ba": row(lru_ba[0]),
        "wx": _block_diag(lru_wx[0], 4), "bx": row(lru_bx[0]),
        "lam": row(lru_lambda[0]),
        "wout1": w_out_odd[0].astype(BF16),
        "fn": row(final_norm),
    }
    return p


def kernel(x_prompt, x_sample, state_ssm, state_ssd_conv, state_ccv, state_lru_conv, state_lru, norm_even, w_in_even, ssd_conv_w, ssd_conv_b, ssd_dt_bias, ssd_a_log, ssd_d, ssd_norm, gmlp_ln_g, gmlp_ln_b, gmlp_w_s, gmlp_b_s, w_out_even, norm_odd, w_in_odd, ccv_w, ccv_b, ccv_ln_g, ccv_ln_b, lru_conv_w, lru_conv_b, lru_wa, lru_ba, lru_wx, lru_bx, lru_lambda, w_out_odd, final_norm):
    p = _prepare(norm_even, w_in_even, ssd_conv_w, ssd_conv_b, ssd_dt_bias, ssd_a_log, ssd_d, ssd_norm,
                 gmlp_ln_g, gmlp_ln_b, gmlp_w_s, gmlp_b_s, w_out_even,
                 norm_odd, w_in_odd, ccv_w, ccv_b, ccv_ln_g, ccv_ln_b,
                 lru_conv_w, lru_conv_b, lru_wa, lru_ba, lru_wx, lru_bx, lru_lambda, w_out_odd, final_norm)
    nb = x_prompt.shape[0]
    ns = x_sample.shape[0]

    x1, ssm_p, sbuf_p = _layer0_prompt(x_prompt, p)
    y_prompt, cbuf_p, lbuf_p, lru_p = _layer1_prompt(x1, p)

    xs = x_sample.reshape(ns, D_MODEL)
    tap_major = lambda s: jnp.transpose(s[0], (1, 0, 2))
    seq_major = lambda s: jnp.transpose(s, (1, 0, 2))[None]
    nsbuf, xdtt, dec, bm, ct, xsd, sz, ybg, vn = _sample0_proj(xs, tap_major(state_ssd_conv), p)
    ssm_s, y_ssd = _sample0_state(dec[:, :A_HEADS], state_ssm[0].reshape(ns, A_HEADS * A_HEAD_DIM, A_STATE),
                                  xdtt, bm, ct)
    x1s, glu, sgc, ydg, nlbuf, lru_s = _sample_mid(y_ssd, xsd, sz, ybg, xs, tap_major(state_lru_conv),
                                                   state_lru[0], p)
    ncbuf, y_sample = _sample1_tail(tap_major(state_ccv), glu, sgc, ydg, x1s, p)

    return (y_prompt,
            y_sample.reshape(ns, 1, D_MODEL),
            ssm_p.reshape(1, nb, A_HEADS, A_HEAD_DIM, A_STATE),
            ssm_s.reshape(1, ns, A_HEADS, A_HEAD_DIM, A_STATE),
            sbuf_p[:, SUB - (A_CONV - 1):, :][None],
            seq_major(nsbuf),
            vn.reshape(1, ns, 1, 1024),
            cbuf_p[:, CPAD - (C_CONV - 1):, :][None],
            seq_major(ncbuf),
            lbuf_p[:, SUB - (D_CONV - 1):, :][None],
            seq_major(nlbuf),
            lru_p[None],
            lru_s[None])
```

```python
import jax
import jax.numpy as jnp
import numpy as np
from jax import lax
from jax.experimental import pallas as pl
from jax.experimental.pallas import tpu as pltpu

F32 = jnp.float32
BF16 = jnp.bfloat16

D_MODEL = 1024
EPS = 1e-6
A_HEADS = 16
A_HEAD_DIM = 64
A_STATE = 128
A_GROUPS = 2
A_CONV = 4
A_CONV_DIM = 1536
CHUNK = 128
B_GROUPS = 8
C_CONV = 31
D_CONV = 4
D_HEADS = 16
D_BLOCK = 64
LRU_C = 8.0

P0_Z, P0_U, P0_V, P0_G, P0_DT, P0_XBC = 0, 1024, 2048, 3072, 4096, 4224
P0_MAIN = 4224
P0_ALL = P0_MAIN + A_CONV_DIM

VMEM_LIMIT_BYTES = 56 * 1024 * 1024

T0 = 512
T1 = 64
LANE = 128
SUB = 8
CPAD = 32


def _mm(a, b):
    return jnp.dot(a, b, preferred_element_type=F32)


def _mm_nt(a, b):
    return lax.dot_general(a, b, (((1,), (1,)), ((), ())), preferred_element_type=F32)


def _sigmoid(x):
    return 0.5 * jnp.tanh(0.5 * x) + 0.5


def _silu(x):
    return x * _sigmoid(x)


def _rms(x, g):
    return x * lax.rsqrt(jnp.mean(x * x, axis=-1, keepdims=True) + EPS) * g


def _ln(x, g, b):
    mu = jnp.mean(x, axis=-1, keepdims=True)
    xc = x - mu
    var = jnp.mean(xc * xc, axis=-1, keepdims=True)
    return xc * lax.rsqrt(var + EPS) * g + b


def _split3(v):
    hi = v.astype(BF16)
    r1 = v - hi.astype(F32)
    mid = r1.astype(BF16)
    lo = (r1 - mid.astype(F32)).astype(BF16)
    return hi, mid, lo


def _mm_exact_rhs(m01, v):
    hi, mid, lo = _split3(v)
    return _mm(m01, hi) + _mm(m01, mid) + _mm(m01, lo)


def _mm_exact_lhs(v, m01):
    hi, mid, lo = _split3(v)
    return _mm(hi, m01) + _mm(mid, m01) + _mm(lo, m01)


def _mm_wide_lhs(v, m01):
    hi = v.astype(BF16)
    mid = (v - hi.astype(F32)).astype(BF16)
    return _mm(hi, m01) + _mm(mid, m01)


def _group_rms(y, gn):
    half = y.shape[-1] // A_GROUPS
    parts = []
    for g in range(A_GROUPS):
        yg = y[:, g * half:(g + 1) * half]
        parts.append(yg * lax.rsqrt(jnp.mean(yg * yg, axis=-1, keepdims=True) + EPS))
    return jnp.concatenate(parts, axis=-1) * gn


def _layer0_prompt_kernel(x_ref, ne_ref, win_ref, cw_ref, cb_ref, dtb_ref, alog_ref, dsk_ref, gn_ref,
                          lng_ref, lnb_ref, ws_ref, bsf_ref, wout_ref, ltri_ref, e64_ref, e128_ref,
                          x1_ref, ssm_ref, cbuf_ref,
                          proj_sc, xpad_sc, xc_sc, mix_sc, state_sc):
    c = pl.program_id(1)
    last = pl.num_programs(1) - 1
    T = x_ref.shape[0]

    @pl.when(c == 0)
    def _():
        state_sc[...] = jnp.zeros_like(state_sc)
        xpad_sc[0:SUB, :] = jnp.zeros((SUB, A_CONV_DIM), F32)

    hn = _rms(x_ref[...], ne_ref[...]).astype(BF16)
    for s in range(0, P0_MAIN, 512):
        e = min(s + 512, P0_MAIN)
        proj_sc[:, s:e] = _mm(hn, win_ref[:, s:e])
    for s in range(0, A_CONV_DIM, 512):
        xpad_sc[SUB:SUB + T, s:s + 512] = _mm(hn, win_ref[:, P0_MAIN + s:P0_MAIN + s + 512])

    acc = cb_ref[...] + cw_ref[A_CONV - 1:A_CONV, :] * xpad_sc[SUB:SUB + T, :]
    for k in range(A_CONV - 1):
        off = SUB - (A_CONV - 1) + k
        acc = acc + cw_ref[k:k + 1, :] * xpad_sc[off:off + T, :]
    xc_sc[...] = _silu(acc)
    tail = xpad_sc[T:T + SUB, :]

    @pl.when(c == last)
    def _():
        cbuf_ref[...] = tail

    xpad_sc[0:SUB, :] = tail

    tril = lax.broadcasted_iota(jnp.int32, (CHUNK, CHUNK), 0) >= lax.broadcasted_iota(jnp.int32, (CHUNK, CHUNK), 1)
    lane_lo = lax.broadcasted_iota(jnp.int32, (CHUNK, LANE), 1) < A_HEAD_DIM
    a_neg = -jnp.exp(alog_ref[...])
    gw = A_HEADS // A_GROUPS * A_HEAD_DIM

    for j in range(T // CHUNK):
        rows = slice(j * CHUNK, (j + 1) * CHUNK)
        xs = xc_sc[rows, 0:1024]
        bm = xc_sc[rows, 1024:1024 + A_GROUPS * A_STATE]
        cm = xc_sc[rows, 1024 + A_GROUPS * A_STATE:A_CONV_DIM]
        dt = jax.nn.softplus(proj_sc[rows, P0_DT:P0_DT + LANE] + dtb_ref[...])
        da = dt * a_neg
        cs = _mm_exact_rhs(ltri_ref[...], da)
        cs_t = cs.T
        dt_e = _mm_wide_lhs(dt, e64_ref[...])
        cs_b = _mm_wide_lhs(cs, e128_ref[...])
        cs_e = jnp.concatenate([jnp.where(lane_lo, cs_b[:, 2 * q * LANE:(2 * q + 1) * LANE],
                                          cs_b[:, (2 * q + 1) * LANE:(2 * q + 2) * LANE])
                                for q in range(A_HEADS // 2)], axis=-1)
        xdt = xs * dt_e
        xdt_bf = xdt.astype(BF16)
        xdec_bf = (xdt * jnp.exp(cs_e[CHUNK - 1:CHUNK, :] - cs_e)).astype(BF16)
        grow = jnp.exp(cs_e)

        y_parts = []
        for g in range(A_GROUPS):
            bg = bm[:, g * A_STATE:(g + 1) * A_STATE].astype(BF16)
            cg = cm[:, g * A_STATE:(g + 1) * A_STATE].astype(BF16)
            cb = lax.dot_general(cg, bg, (((1,), (1,)), ((), ())), preferred_element_type=F32)
            st = state_sc[g * gw:(g + 1) * gw, :]
            y_off = lax.dot_general(cg, st.astype(BF16), (((1,), (1,)), ((), ())), preferred_element_type=F32)
            upd = lax.dot_general(xdec_bf[:, g * gw:(g + 1) * gw], bg, (((0,), (0,)), ((), ())),
                                  preferred_element_type=F32)
            for hp in range(A_HEADS // A_GROUPS // 2):
                res = []
                for q in range(2):
                    h = g * (A_HEADS // A_GROUPS) + 2 * hp + q
                    diff = cs_b[:, h * LANE:(h + 1) * LANE] - cs_t[h:h + 1, :]
                    lm = jnp.exp(jnp.where(tril, diff, -jnp.inf))
                    wm = (cb * lm).astype(BF16)
                    res.append(_mm(wm, xdt_bf[:, g * gw + hp * LANE:g * gw + (hp + 1) * LANE]))
                    r_lo = g * gw + (2 * hp + q) * A_HEAD_DIM
                    state_sc[r_lo:r_lo + A_HEAD_DIM, :] = (
                        jnp.exp(cs_t[h:h + 1, CHUNK - 1:CHUNK]) * st[(2 * hp + q) * A_HEAD_DIM:(2 * hp + q + 1) * A_HEAD_DIM, :]
                        + upd[(2 * hp + q) * A_HEAD_DIM:(2 * hp + q + 1) * A_HEAD_DIM, :])
                y_parts.append(jnp.where(lane_lo, res[0], res[1]) + y_off[:, hp * LANE:(hp + 1) * LANE]
                               * grow[:, g * gw + hp * LANE:g * gw + (hp + 1) * LANE])
        y = jnp.concatenate(y_parts, axis=-1) + xs * dsk_ref[...]
        ya = _group_rms(y * _silu(proj_sc[rows, P0_Z:P0_Z + 1024]), gn_ref[...])
        mix_sc[rows, 0:1024] = ya.astype(BF16)

        ug = jax.nn.gelu(proj_sc[rows, P0_U:P0_U + 1024])
        vn = _ln(jax.nn.gelu(proj_sc[rows, P0_V:P0_V + 1024]), lng_ref[...], lnb_ref[...])
        vn_bf = vn.astype(BF16)
        mixed = []
        for g in range(B_GROUPS):
            wt = jnp.where(tril, ws_ref[g], 0.0).astype(BF16)
            mixed.append(_mm(wt, vn_bf[:, g * LANE:(g + 1) * LANE]))
        mixed = jnp.concatenate(mixed, axis=-1) + bsf_ref[...]
        yb = ug * mixed * _silu(proj_sc[rows, P0_G:P0_G + 1024])
        mix_sc[rows, 1024:2048] = yb.astype(BF16)

    x1_ref[...] = x_ref[...] + _mm(mix_sc[...], wout_ref[...])

    @pl.when(c == last)
    def _():
        ssm_ref[...] = state_sc[...]


def _layer0_prompt(x, p):
    nb, seq, _ = x.shape
    T = T0
    const = lambda shape: pl.BlockSpec(shape, lambda b, c: (0,) * len(shape), pipeline_mode=pl.Buffered(1))
    in_specs = [
        pl.BlockSpec((None, T, D_MODEL), lambda b, c: (b, c, 0)),
        const((1, D_MODEL)), const((D_MODEL, P0_ALL)), const((A_CONV, A_CONV_DIM)), const((1, A_CONV_DIM)),
        const((1, LANE)), const((1, LANE)), const((1, 1024)), const((1, 1024)),
        const((1, 1024)), const((1, 1024)), const((B_GROUPS, CHUNK, CHUNK)), const((CHUNK, 1024)),
        const((2048, D_MODEL)), const((CHUNK, CHUNK)), const((LANE, 1024)), const((LANE, 2048)),
    ]
    out_shape = (jax.ShapeDtypeStruct((nb, seq, D_MODEL), F32),
                 jax.ShapeDtypeStruct((nb, A_HEADS * A_HEAD_DIM, A_STATE), F32),
                 jax.ShapeDtypeStruct((nb, SUB, A_CONV_DIM), F32))
    out_specs = (pl.BlockSpec((None, T, D_MODEL), lambda b, c: (b, c, 0)),
                 pl.BlockSpec((None, A_HEADS * A_HEAD_DIM, A_STATE), lambda b, c: (b, 0, 0)),
                 pl.BlockSpec((None, SUB, A_CONV_DIM), lambda b, c: (b, 0, 0)))
    scratch = [pltpu.VMEM((T, P0_MAIN), F32), pltpu.VMEM((SUB + T, A_CONV_DIM), F32),
               pltpu.VMEM((T, A_CONV_DIM), F32), pltpu.VMEM((T, 2048), BF16),
               pltpu.VMEM((A_HEADS * A_HEAD_DIM, A_STATE), F32)]
    return pl.pallas_call(
        _layer0_prompt_kernel, out_shape=out_shape, grid=(nb, seq // T),
        in_specs=in_specs, out_specs=out_specs, scratch_shapes=scratch,
        compiler_params=pltpu.CompilerParams(dimension_semantics=("arbitrary", "arbitrary"),
                                             vmem_limit_bytes=VMEM_LIMIT_BYTES),
        name="layer0_prompt",
    )(x, p["ne"], p["win0"], p["ssd_cw"], p["ssd_cb"], p["dtb"], p["alog"], p["dsk"], p["gn"],
      p["lng"], p["lnb"], p["ws"], p["bsf"], p["wout0"], p["ltri"], p["e64"], p["e128"])


def _lru_gates(xc, wa_ref, ba_ref, wx_ref, bx_ref, lam_ref):
    xb = xc.astype(BF16)
    nblk = wa_ref.shape[0]
    w = wa_ref.shape[1]
    r = jnp.concatenate([_mm(xb[:, j * w:(j + 1) * w], wa_ref[j]) for j in range(nblk)], axis=-1)
    i = jnp.concatenate([_mm(xb[:, j * w:(j + 1) * w], wx_ref[j]) for j in range(nblk)], axis=-1)
    r = _sigmoid(r + ba_ref[...])
    i = _sigmoid(i + bx_ref[...])
    log_a = -LRU_C * r * jax.nn.softplus(-lam_ref[...])
    a = jnp.exp(log_a)
    bt = jnp.sqrt(1.0 - a * a) * (i * xc)
    return a, bt


def _dwconv_rows(pad_ref, w_ref, n_taps, first_row, n_rows, row_step):
    cols = []
    blk = 8 * SUB
    for j in range(pad_ref.shape[1] // LANE):
        lanes = slice(j * LANE, (j + 1) * LANE)
        blocks = []
        for r0 in range(0, n_rows, blk):
            acc = None
            for k in range(n_taps):
                lo = first_row + k * row_step + r0
                term = w_ref[k:k + 1, lanes] * pad_ref[lo:lo + blk, lanes]
                acc = term if acc is None else acc + term
            blocks.append(acc)
        cols.append(jnp.concatenate(blocks, axis=0))
    return jnp.concatenate(cols, axis=-1)


def _layer1_prompt_kernel(x_ref, no_ref, win_ref, ccw_ref, ccb_ref, clg_ref, clb_ref, lcw_ref, lcb_ref,
                          wa_ref, ba_ref, wx_ref, bx_ref, lam_ref, wout_ref, fn_ref,
                          y_ref, ccv_ref, lcv_ref, lru_ref,
                          cpad_sc, lpad_sc, h_sc, mix_sc):
    c = pl.program_id(0)
    last = pl.num_programs(0) - 1
    nb, T, _ = x_ref.shape
    rows_all = nb * T
    chist = CPAD * nb
    lhist = SUB * nb

    @pl.when(c == 0)
    def _():
        cpad_sc[0:chist, :] = jnp.zeros((chist, D_MODEL), F32)
        lpad_sc[0:lhist, :] = jnp.zeros((lhist, D_MODEL), F32)
        h_sc[...] = jnp.zeros_like(h_sc)

    TS = T // 2
    rows_all = nb * TS
    for t0 in range(0, T, TS):
        x = jnp.concatenate([x_ref[:, t0 + t, :] for t in range(TS)], axis=0)
        hn = _rms(x, no_ref[...]).astype(BF16)
        cpad_sc[chist:chist + rows_all, :] = (_mm(hn, win_ref[:, 0:1024])
                                              * _sigmoid(_mm(hn, win_ref[:, 1024:2048])))
        lpad_sc[lhist:lhist + rows_all, :] = _mm(hn, win_ref[:, 3072:4096])
        sgc = _silu(_mm(hn, win_ref[:, 2048:3072]))
        sgd = _silu(_mm(hn, win_ref[:, 4096:5120]))

        cacc = _dwconv_rows(cpad_sc, ccw_ref, C_CONV, (CPAD - (C_CONV - 1)) * nb, rows_all, nb) + ccb_ref[...]
        mix_sc[:, 0:1024] = (_silu(_ln(cacc, clg_ref[...], clb_ref[...])) * sgc).astype(BF16)

        xc = _dwconv_rows(lpad_sc, lcw_ref, D_CONV, (SUB - (D_CONV - 1)) * nb, rows_all, nb) + lcb_ref[...]
        a, bt = _lru_gates(xc, wa_ref, ba_ref, wx_ref, bx_ref, lam_ref)
        h = h_sc[...]
        hs = []
        for t in range(TS):
            h = a[t * nb:(t + 1) * nb, :] * h + bt[t * nb:(t + 1) * nb, :]
            hs.append(h)
        h_sc[...] = h
        mix_sc[:, 1024:2048] = (jnp.concatenate(hs, axis=0) * sgd).astype(BF16)

        y = _rms(x + _mm(mix_sc[...], wout_ref[...]), fn_ref[...])
        for t in range(TS):
            y_ref[:, t0 + t, :] = y[t * nb:(t + 1) * nb, :]
        cpad_sc[0:chist, :] = cpad_sc[rows_all:rows_all + chist, :]
        lpad_sc[0:lhist, :] = lpad_sc[rows_all:rows_all + lhist, :]

    @pl.when(c == last)
    def _():
        for r in range(CPAD):
            ccv_ref[:, r, :] = cpad_sc[r * nb:(r + 1) * nb, :]
        for r in range(SUB):
            lcv_ref[:, r, :] = lpad_sc[r * nb:(r + 1) * nb, :]
        lru_ref[...] = h_sc[...]


def _layer1_prompt(x1, p):
    nb, seq, _ = x1.shape
    T = T1
    rows_all = nb * T
    const = lambda shape: pl.BlockSpec(shape, lambda c: (0,) * len(shape), pipeline_mode=pl.Buffered(1))
    nblk, wblk = p["wa"].shape[0], p["wa"].shape[1]
    in_specs = [
        pl.BlockSpec((nb, T, D_MODEL), lambda c: (0, c, 0)),
        const((1, D_MODEL)), const((D_MODEL, 5120)), const((C_CONV, 1024)), const((1, 1024)), const((1, 1024)),
        const((1, 1024)), const((D_CONV, 1024)), const((1, 1024)),
        const((nblk, wblk, wblk)), const((1, 1024)), const((nblk, wblk, wblk)), const((1, 1024)), const((1, 1024)),
        const((2048, D_MODEL)), const((1, D_MODEL)),
    ]
    out_shape = (jax.ShapeDtypeStruct((nb, seq, D_MODEL), F32),
                 jax.ShapeDtypeStruct((nb, CPAD, 1024), F32),
                 jax.ShapeDtypeStruct((nb, SUB, 1024), F32),
                 jax.ShapeDtypeStruct((nb, 1024), F32))
    out_specs = (pl.BlockSpec((nb, T, D_MODEL), lambda c: (0, c, 0)),
                 pl.BlockSpec((nb, CPAD, 1024), lambda c: (0, 0, 0)),
                 pl.BlockSpec((nb, SUB, 1024), lambda c: (0, 0, 0)),
                 pl.BlockSpec((nb, 1024), lambda c: (0, 0)))
    scratch = [pltpu.VMEM(((CPAD + T // 2) * nb, 1024), F32), pltpu.VMEM(((SUB + T // 2) * nb, 1024), F32),
               pltpu.VMEM((nb, 1024), F32), pltpu.VMEM((rows_all // 2, 2048), BF16)]
    return pl.pallas_call(
        _layer1_prompt_kernel, out_shape=out_shape, grid=(seq // T,),
        in_specs=in_specs, out_specs=out_specs, scratch_shapes=scratch,
        compiler_params=pltpu.CompilerParams(dimension_semantics=("arbitrary",),
                                             vmem_limit_bytes=VMEM_LIMIT_BYTES),
        name="layer1_prompt",
    )(x1, p["no"], p["win1"], p["ccw"], p["ccb"], p["clg"], p["clb"], p["lcw"], p["lcb"],
      p["wa"], p["ba"], p["wx"], p["bx"], p["lam"], p["wout1"], p["fn"])


def _sample0_proj_kernel(x_ref, ne_ref, win_ref, sbuf_ref, cw_ref, cb_ref, dtb_ref, alog_ref, dsk_ref,
                         lng_ref, lnb_ref, w00_ref, b00_ref, e64_ref,
                         nbuf_ref, xdtt_ref, dec_ref, bm_ref, ct_ref, xsd_ref, sz_ref, ybg_ref, vn_ref):
    hn = _rms(x_ref[...], ne_ref[...]).astype(BF16)
    xbc = _mm(hn, win_ref[:, P0_MAIN:P0_ALL])
    acc = cb_ref[...] + cw_ref[A_CONV - 1:A_CONV, :] * xbc
    for k in range(A_CONV - 1):
        acc = acc + cw_ref[k:k + 1, :] * sbuf_ref[k]
    for k in range(A_CONV - 2):
        nbuf_ref[k] = sbuf_ref[k + 1]
    nbuf_ref[A_CONV - 2] = xbc
    xcv = _silu(acc)
    xs = xcv[:, 0:1024]
    nbc = A_GROUPS * A_STATE
    bm_ref[...] = xcv[:, 1024:1024 + nbc]
    ct_ref[...] = xcv[:, 1024 + nbc:A_CONV_DIM].T
    dt = jax.nn.softplus(_mm(hn, win_ref[:, P0_DT:P0_DT + LANE]) + dtb_ref[...])
    dec = jnp.exp(dt * (-jnp.exp(alog_ref[...])))
    xdtt_ref[...] = (xs * _mm_exact_lhs(dt, e64_ref[...])).T.astype(BF16)
    dec_ref[...] = dec
    xsd_ref[...] = xs * dsk_ref[...]
    sz_ref[...] = _silu(_mm(hn, win_ref[:, P0_Z:P0_Z + 1024]))
    vn = _ln(jax.nn.gelu(_mm(hn, win_ref[:, P0_V:P0_V + 1024])), lng_ref[...], lnb_ref[...])
    vn_ref[...] = vn
    mixed = w00_ref[...] * vn + b00_ref[...]
    ybg_ref[...] = (jax.nn.gelu(_mm(hn, win_ref[:, P0_U:P0_U + 1024])) * mixed
                    * _silu(_mm(hn, win_ref[:, P0_G:P0_G + 1024])))


def _sample0_proj(xs, sbuf, p):
    n = xs.shape[0]
    full = lambda a: pl.BlockSpec(a.shape, lambda i: (0,) * a.ndim)
    args = (xs, p["ne"], p["win0"], sbuf, p["ssd_cw"], p["ssd_cb"], p["dtb"], p["alog"], p["dsk"],
            p["lng"], p["lnb"], p["w00"], p["b00"], p["e64"])
    nbc = A_GROUPS * A_STATE
    shapes = [sbuf.shape, (1024, n), (n, LANE), (n, nbc), (nbc, n), (n, 1024), (n, 1024),
              (n, 1024), (n, 1024)]
    dtypes = [F32, BF16] + [F32] * 7
    return pl.pallas_call(
        _sample0_proj_kernel, grid=(1,),
        out_shape=tuple(jax.ShapeDtypeStruct(s, d) for s, d in zip(shapes, dtypes)),
        in_specs=[full(a) for a in args],
        out_specs=tuple(pl.BlockSpec(s, lambda i, nd=len(s): (0,) * nd) for s in shapes),
        compiler_params=pltpu.CompilerParams(dimension_semantics=("arbitrary",), vmem_limit_bytes=VMEM_LIMIT_BYTES),
        name="sample0_proj",
    )(*args)


def _sample0_state_kernel(dec_ref, h_ref, xt_ref, bm_ref, ct_ref, hn_ref, yt_ref):
    i = pl.program_id(0)
    tb = h_ref.shape[0]
    ntok = bm_ref.shape[0]
    hpg = A_HEADS // A_GROUPS
    gw = hpg * A_HEAD_DIM

    @pl.when(i == 0)
    def _():
        yt_ref[...] = jnp.zeros_like(yt_ref)

    row_id = lax.broadcasted_iota(jnp.int32, (ntok, A_STATE), 0)
    lane_id = lax.broadcasted_iota(jnp.int32, (A_STATE, ntok), 1)
    acc = [jnp.zeros((gw, ntok), F32) for _ in range(A_GROUPS)]
    for b in range(tb):
        tok = i * tb + b
        for g in range(A_GROUPS):
            bmask = jnp.where(row_id == tok, bm_ref[:, g * A_STATE:(g + 1) * A_STATE], 0.0).astype(BF16)
            upd = _mm(xt_ref[g * gw:(g + 1) * gw, :], bmask)
            parts = []
            for h in range(hpg):
                lo = g * gw + h * A_HEAD_DIM
                hn = dec_ref[tok, g * hpg + h] * h_ref[b, lo:lo + A_HEAD_DIM, :] + upd[h * A_HEAD_DIM:(h + 1) * A_HEAD_DIM, :]
                hn_ref[b, lo:lo + A_HEAD_DIM, :] = hn
                parts.append(hn)
            cmask = jnp.where(lane_id == tok, ct_ref[g * A_STATE:(g + 1) * A_STATE, :], 0.0).astype(BF16)
            acc[g] = acc[g] + _mm(jnp.concatenate(parts, axis=0).astype(BF16), cmask)
    for g in range(A_GROUPS):
        yt_ref[g * gw:(g + 1) * gw, :] += acc[g]


def _sample0_state(dec, h0, xdtt, bm, ct):
    n = h0.shape[0]
    tb = SUB
    rows = A_HEADS * A_HEAD_DIM
    blk = pl.BlockSpec((tb, rows, A_STATE), lambda i: (i, 0, 0))
    full = lambda a: pl.BlockSpec(a.shape, lambda i: (0,) * a.ndim)
    return pl.pallas_call(
        _sample0_state_kernel, grid=(n // tb,),
        out_shape=(jax.ShapeDtypeStruct(h0.shape, F32), jax.ShapeDtypeStruct((rows, n), F32)),
        in_specs=[pl.BlockSpec(memory_space=pltpu.SMEM), blk, full(xdtt), full(bm), full(ct)],
        out_specs=(blk, pl.BlockSpec((rows, n), lambda i: (0, 0))),
        compiler_params=pltpu.CompilerParams(dimension_semantics=("arbitrary",), vmem_limit_bytes=VMEM_LIMIT_BYTES),
        name="sample0_state",
    )(dec, h0, xdtt, bm, ct)


def _sample_mid_kernel(yt_ref, xsd_ref, sz_ref, ybg_ref, x_ref, gn_ref, wout0_ref,
                       no_ref, win_ref, lbuf_ref, h0_ref, lcw_ref, lcb_ref,
                       wa_ref, ba_ref, wx_ref, bx_ref, lam_ref,
                       x1_ref, glu_ref, sgc_ref, ydg_ref, nlbuf_ref, hnew_ref):
    ya = _group_rms((yt_ref[...].T + xsd_ref[...]) * sz_ref[...], gn_ref[...])
    mix = jnp.concatenate([ya, ybg_ref[...]], axis=-1).astype(BF16)
    x1 = x_ref[...] + _mm(mix, wout0_ref[...])
    x1_ref[...] = x1
    hn = _rms(x1, no_ref[...]).astype(BF16)
    glu_ref[...] = _mm(hn, win_ref[:, 0:1024]) * _sigmoid(_mm(hn, win_ref[:, 1024:2048]))
    sgc_ref[...] = _silu(_mm(hn, win_ref[:, 2048:3072]))
    xd = _mm(hn, win_ref[:, 3072:4096])
    acc = lcb_ref[...] + lcw_ref[D_CONV - 1:D_CONV, :] * xd
    for k in range(D_CONV - 1):
        acc = acc + lcw_ref[k:k + 1, :] * lbuf_ref[k]
    for k in range(D_CONV - 2):
        nlbuf_ref[k] = lbuf_ref[k + 1]
    nlbuf_ref[D_CONV - 2] = xd
    a, bt = _lru_gates(acc, wa_ref, ba_ref, wx_ref, bx_ref, lam_ref)
    h = a * h0_ref[...] + bt
    hnew_ref[...] = h
    ydg_ref[...] = h * _silu(_mm(hn, win_ref[:, 4096:5120]))


def _sample_mid(yt, xsd, sz, ybg, xs, lbuf, h0, p):
    n = xs.shape[0]
    full = lambda a: pl.BlockSpec(a.shape, lambda i: (0,) * a.ndim)
    args = (yt, xsd, sz, ybg, xs, p["gn"], p["wout0"], p["no"], p["win1"], lbuf, h0, p["lcw"], p["lcb"],
            p["wa"], p["ba"], p["wx"], p["bx"], p["lam"])
    shapes = [(n, 1024), (n, 1024), (n, 1024), (n, 1024), lbuf.shape, (n, 1024)]
    return pl.pallas_call(
        _sample_mid_kernel, grid=(1,),
        out_shape=tuple(jax.ShapeDtypeStruct(s, F32) for s in shapes),
        in_specs=[full(a) for a in args],
        out_specs=tuple(pl.BlockSpec(s, lambda i, nd=len(s): (0,) * nd) for s in shapes),
        compiler_params=pltpu.CompilerParams(dimension_semantics=("arbitrary",), vmem_limit_bytes=VMEM_LIMIT_BYTES),
        name="sample_mid",
    )(*args)


def _sample1_tail_kernel(cbuf_ref, glu_ref, sgc_ref, ydg_ref, x1_ref, ccw_ref, ccb_ref, clg_ref, clb_ref,
                         wout_ref, fn_ref, ncbuf_ref, y_ref):
    hist = C_CONV - 1
    glu = glu_ref[...]
    acc = ccb_ref[...] + ccw_ref[hist:C_CONV, :] * glu
    for k in range(hist):
        acc = acc + ccw_ref[k:k + 1, :] * cbuf_ref[k]
    for k in range(hist - 1):
        ncbuf_ref[k] = cbuf_ref[k + 1]
    ncbuf_ref[hist - 1] = glu
    yc =_silu(_ln(acc, clg_ref[...], clb_ref[...])) * sgc_ref[...]
    mix = jnp.concatenate([yc, ydg_ref[...]], axis=-1).astype(BF16)
    out = x1_ref[...] + _mm(mix, wout_ref[...])
    y_ref[...] = _rms(out, fn_ref[...])


def _sample1_tail(cbuf, glu, sgc, ydg, x1, p):
    n = x1.shape[0]
    tb = 32
    row = lambda w: pl.BlockSpec((tb, w), lambda i: (i, 0))
    blk = pl.BlockSpec((cbuf.shape[0], tb, cbuf.shape[2]), lambda i: (0, i, 0))
    const = lambda a: pl.BlockSpec(a.shape, lambda i: (0,) * a.ndim, pipeline_mode=pl.Buffered(1))
    consts = (p["ccw"], p["ccb"], p["clg"], p["clb"], p["wout1"], p["fn"])
    return pl.pallas_call(
        _sample1_tail_kernel, grid=(n // tb,),
        out_shape=(jax.ShapeDtypeStruct(cbuf.shape, F32), jax.ShapeDtypeStruct((n, 1024), F32)),
        in_specs=[blk, row(1024), row(1024), row(1024), row(1024)] + [const(a) for a in consts],
        out_specs=(blk, row(1024)),
        compiler_params=pltpu.CompilerParams(dimension_semantics=("arbitrary",), vmem_limit_bytes=VMEM_LIMIT_BYTES),
        name="sample1_tail",
    )(cbuf, glu, sgc, ydg, x1, *consts)


def _block_diag(w, per_block):
    nh, d, _ = w.shape
    nblk = nh // per_block
    w = w.reshape(nblk, per_block, d, d)
    eye = jnp.eye(per_block, dtype=w.dtype)
    bd = jnp.einsum("nhij,hk->nhikj", w, eye)
    return bd.reshape(nblk, per_block * d, per_block * d).astype(BF16)


def _repack_w_in_even_kernel(w_ref, o_ref):
    n_xbc_end = 1024 + A_CONV_DIM
    cb = w_ref.shape[1]
    eye = (lax.broadcasted_iota(jnp.int32, (cb, cb), 0) == lax.broadcasted_iota(jnp.int32, (cb, cb), 1)).astype(BF16)
    tr = lambda piece: _mm_nt(eye, piece.astype(BF16)).astype(BF16)
    o_ref[:, P0_Z:P0_U] = tr(w_ref[0:1024, :])
    o_ref[:, P0_U:P0_DT] = tr(w_ref[n_xbc_end + A_HEADS:, :])
    o_ref[:, P0_DT:P0_XBC] = tr(jnp.concatenate([w_ref[n_xbc_end:n_xbc_end + A_HEADS, :],
                                                 jnp.zeros((LANE - A_HEADS, cb), F32)], axis=0))
    o_ref[:, P0_XBC:P0_ALL] = tr(w_ref[1024:n_xbc_end, :])


def _repack_w_in_even(w_t):
    rows, cols = w_t.shape
    cb = 2 * LANE
    return pl.pallas_call(
        _repack_w_in_even_kernel, out_shape=jax.ShapeDtypeStruct((cols, P0_ALL), BF16), grid=(cols // cb,),
        in_specs=[pl.BlockSpec((rows, cb), lambda i: (0, i))],
        out_specs=pl.BlockSpec((cb, P0_ALL), lambda i: (i, 0)),
        compiler_params=pltpu.CompilerParams(dimension_semantics=("arbitrary",), vmem_limit_bytes=VMEM_LIMIT_BYTES),
        name="repack_w_in_even",
    )(w_t)


def _prepare(norm_even, w_in_even, ssd_conv_w, ssd_conv_b, ssd_dt_bias, ssd_a_log, ssd_d, ssd_norm,
             gmlp_ln_g, gmlp_ln_b, gmlp_w_s, gmlp_b_s, w_out_even,
             norm_odd, w_in_odd, ccv_w, ccv_b, ccv_ln_g, ccv_ln_b,
             lru_conv_w, lru_conv_b, lru_wa, lru_ba, lru_wx, lru_bx, lru_lambda, w_out_odd, final_norm):
    row = lambda v: v.reshape(1, -1).astype(F32)
    heads = np.arange(LANE)[:, None]
    p = {
        "ne": row(norm_even[0]),
        "win0": _repack_w_in_even(jnp.transpose(w_in_even[0])),
        "ssd_cw": ssd_conv_w[0], "ssd_cb": row(ssd_conv_b[0]),
        "dtb": jnp.pad(row(ssd_dt_bias[0]), ((0, 0), (0, LANE - A_HEADS))),
        "alog": jnp.pad(row(ssd_a_log[0]), ((0, 0), (0, LANE - A_HEADS))),
        "dsk": row(jnp.repeat(ssd_d[0], A_HEAD_DIM)),
        "gn": row(ssd_norm[0]),
        "lng": row(gmlp_ln_g[0]), "lnb": row(gmlp_ln_b[0]),
        "ws": gmlp_w_s[0],
        "bsf": jnp.repeat(gmlp_b_s[0].T, LANE, axis=1),
        "w00": row(jnp.repeat(gmlp_w_s[0, :, 0, 0], LANE)), "b00": row(jnp.repeat(gmlp_b_s[0, :, 0], LANE)),
        "wout0": w_out_even[0].astype(BF16),
        "ltri": jnp.asarray(np.arange(CHUNK)[:, None] >= np.arange(CHUNK)[None, :], BF16),
        "e64": jnp.asarray(np.arange(1024)[None, :] // A_HEAD_DIM == heads, BF16),
        "e128": jnp.asarray(np.arange(2048)[None, :] // LANE == heads, BF16),
        "no": row(norm_odd[0]),
        "win1": w_in_odd[0].astype(BF16),
        "ccw": ccv_w[0], "ccb": row(ccv_b[0]), "clg": row(ccv_ln_g[0]), "clb": row(ccv_ln_b[0]),
        "lcw": lru_conv_w[0], "lcb": row(lru_conv_b[0]),
        "wa": _block_diag(lru_wa[0], 4), "ba": row(lru_ba[0]),
        "wx": _block_diag(lru_wx[0], 4), "bx": row(lru_bx[0]),
        "lam": row(lru_lambda[0]),
        "wout1": w_out_odd[0].astype(BF16),
        "fn": row(final_norm),
    }
    return p


def kernel(x_prompt, x_sample, state_ssm, state_ssd_conv, state_ccv, state_lru_conv, state_lru, norm_even, w_in_even, ssd_conv_w, ssd_conv_b, ssd_dt_bias, ssd_a_log, ssd_d, ssd_norm, gmlp_ln_g, gmlp_ln_b, gmlp_w_s, gmlp_b_s, w_out_even, norm_odd, w_in_odd, ccv_w, ccv_b, ccv_ln_g, ccv_ln_b, lru_conv_w, lru_conv_b, lru_wa, lru_ba, lru_wx, lru_bx, lru_lambda, w_out_odd, final_norm):
    p = _prepare(norm_even, w_in_even, ssd_conv_w, ssd_conv_b, ssd_dt_bias, ssd_a_log, ssd_d, ssd_norm,
                 gmlp_ln_g, gmlp_ln_b, gmlp_w_s, gmlp_b_s, w_out_even,
                 norm_odd, w_in_odd, ccv_w, ccv_b, ccv_ln_g, ccv_ln_b,
                 lru_conv_w, lru_conv_b, lru_wa, lru_ba, lru_wx, lru_bx, lru_lambda, w_out_odd, final_norm)
    nb = x_prompt.shape[0]
    ns = x_sample.shape[0]

    x1, ssm_p, sbuf_p = _layer0_prompt(x_prompt, p)
    y_prompt, cbuf_p, lbuf_p, lru_p = _layer1_prompt(x1, p)

    xs = x_sample.reshape(ns, D_MODEL)
    tap_major = lambda s: jnp.transpose(s[0], (1, 0, 2))
    seq_major = lambda s: jnp.transpose(s, (1, 0, 2))[None]
    nsbuf, xdtt, dec, bm, ct, xsd, sz, ybg, vn = _sample0_proj(xs, tap_major(state_ssd_conv), p)
    ssm_s, y_ssd = _sample0_state(dec[:, :A_HEADS], state_ssm[0].reshape(ns, A_HEADS * A_HEAD_DIM, A_STATE),
                                  xdtt, bm, ct)
    x1s, glu, sgc, ydg, nlbuf, lru_s = _sample_mid(y_ssd, xsd, sz, ybg, xs, tap_major(state_lru_conv),
                                                   state_lru[0], p)
    ncbuf, y_sample = _sample1_tail(tap_major(state_ccv), glu, sgc, ydg, x1s, p)

    return (y_prompt,
            y_sample.reshape(ns, 1, D_MODEL),
            ssm_p.reshape(1, nb, A_HEADS, A_HEAD_DIM, A_STATE),
            ssm_s.reshape(1, ns, A_HEADS, A_HEAD_DIM, A_STATE),
            sbuf_p[:, SUB - (A_CONV - 1):, :][None],
            seq_major(nsbuf),
            vn.reshape(1, ns, 1, 1024),
            cbuf_p[:, CPAD - (C_CONV - 1):, :][None],
            seq_major(ncbuf),
            lbuf_p[:, SUB - (D_CONV - 1):, :][None],
            seq_major(nlbuf),
            lru_p[None],
            lru_s[None])
```

```python
import jax
import jax.numpy as jnp
import numpy as np
from jax import lax
from jax.experimental import pallas as pl
from jax.experimental.pallas import tpu as pltpu

F32 = jnp.float32
BF16 = jnp.bfloat16

D_MODEL = 1024
EPS = 1e-6
A_HEADS = 16
A_HEAD_DIM = 64
A_STATE = 128
A_GROUPS = 2
A_CONV = 4
A_CONV_DIM = 1536
CHUNK = 128
B_GROUPS = 8
C_CONV = 31
D_CONV = 4
D_HEADS = 16
D_BLOCK = 64
LRU_C = 8.0

P0_Z, P0_U, P0_V, P0_G, P0_DT, P0_XBC = 0, 1024, 2048, 3072, 4096, 4224
P0_MAIN = 4224
P0_ALL = P0_MAIN + A_CONV_DIM

VMEM_LIMIT_BYTES = 56 * 1024 * 1024

T0 = 512
T1 = 64
LANE = 128
SUB = 8
CPAD = 32


def _mm(a, b):
    return jnp.dot(a, b, preferred_element_type=F32)


def _mm_nt(a, b):
    return lax.dot_general(a, b, (((1,), (1,)), ((), ())), preferred_element_type=F32)


def _sigmoid(x):
    return 0.5 * jnp.tanh(0.5 * x) + 0.5


def _silu(x):
    return x * _sigmoid(x)


def _rms(x, g):
    return x * lax.rsqrt(jnp.mean(x * x, axis=-1, keepdims=True) + EPS) * g


def _ln(x, g, b):
    mu = jnp.mean(x, axis=-1, keepdims=True)
    xc = x - mu
    var = jnp.mean(xc * xc, axis=-1, keepdims=True)
    return xc * lax.rsqrt(var + EPS) * g + b


def _split3(v):
    hi = v.astype(BF16)
    r1 = v - hi.astype(F32)
    mid = r1.astype(BF16)
    lo = (r1 - mid.astype(F32)).astype(BF16)
    return hi, mid, lo


def _mm_exact_rhs(m01, v):
    hi, mid, lo = _split3(v)
    return _mm(m01, hi) + _mm(m01, mid) + _mm(m01, lo)


def _mm_exact_lhs(v, m01):
    hi, mid, lo = _split3(v)
    return _mm(hi, m01) + _mm(mid, m01) + _mm(lo, m01)


def _mm_wide_lhs(v, m01):
    hi = v.astype(BF16)
    mid = (v - hi.astype(F32)).astype(BF16)
    return _mm(hi, m01) + _mm(mid, m01)


def _group_rms(y, gn):
    half = y.shape[-1] // A_GROUPS
    parts = []
    for g in range(A_GROUPS):
        yg = y[:, g * half:(g + 1) * half]
        parts.append(yg * lax.rsqrt(jnp.mean(yg * yg, axis=-1, keepdims=True) + EPS))
    return jnp.concatenate(parts, axis=-1) * gn


def _layer0_prompt_kernel(x_ref, ne_ref, win_ref, cw_ref, cb_ref, dtb_ref, alog_ref, dsk_ref, gn_ref,
                          lng_ref, lnb_ref, ws_ref, bsf_ref, wout_ref, ltri_ref, e64_ref, e128_ref,
                          x1_ref, ssm_ref, cbuf_ref,
                          proj_sc, xpad_sc, xc_sc, mix_sc, state_sc):
    c = pl.program_id(1)
    last = pl.num_programs(1) - 1
    T = x_ref.shape[0]

    @pl.when(c == 0)
    def _():
        state_sc[...] = jnp.zeros_like(state_sc)
        xpad_sc[0:SUB, :] = jnp.zeros((SUB, A_CONV_DIM), F32)

    hn = _rms(x_ref[...], ne_ref[...]).astype(BF16)
    for s in range(0, A_CONV_DIM, 512):
        xpad_sc[SUB:SUB + T, s:s + 512] = _mm(hn, win_ref[:, P0_MAIN + s:P0_MAIN + s + 512])
    for s, e in ((P0_DT, P0_MAIN), (P0_Z, P0_Z + 512), (P0_Z + 512, P0_U)):
        proj_sc[:, s:e] = _mm(hn, win_ref[:, s:e])
    for s in range(P0_U, P0_DT, 512):
        proj_sc[:, s:s + 512] = _mm(hn, win_ref[:, s:s + 512])

    acc = cb_ref[...] + cw_ref[A_CONV - 1:A_CONV, :] * xpad_sc[SUB:SUB + T, :]
    for k in range(A_CONV - 1):
        off = SUB - (A_CONV - 1) + k
        acc = acc + cw_ref[k:k + 1, :] * xpad_sc[off:off + T, :]
    xc_sc[...] = _silu(acc)
    tail = xpad_sc[T:T + SUB, :]

    @pl.when(c == last)
    def _():
        cbuf_ref[...] = tail

    xpad_sc[0:SUB, :] = tail

    tril = lax.broadcasted_iota(jnp.int32, (CHUNK, CHUNK), 0) >= lax.broadcasted_iota(jnp.int32, (CHUNK, CHUNK), 1)
    lane_lo = lax.broadcasted_iota(jnp.int32, (CHUNK, LANE), 1) < A_HEAD_DIM
    a_neg = -jnp.exp(alog_ref[...])
    gw = A_HEADS // A_GROUPS * A_HEAD_DIM

    for j in range(T // CHUNK):
        rows = slice(j * CHUNK, (j + 1) * CHUNK)
        xs = xc_sc[rows, 0:1024]
        bm = xc_sc[rows, 1024:1024 + A_GROUPS * A_STATE]
        cm = xc_sc[rows, 1024 + A_GROUPS * A_STATE:A_CONV_DIM]
        dt = jax.nn.softplus(proj_sc[rows, P0_DT:P0_DT + LANE] + dtb_ref[...])
        da = dt * a_neg
        cs = _mm_exact_rhs(ltri_ref[...], da)
        cs_t = cs.T
        dt_e = _mm_wide_lhs(dt, e64_ref[...])
        cs_b = _mm_wide_lhs(cs, e128_ref[...])
        cs_e = jnp.concatenate([jnp.where(lane_lo, cs_b[:, 2 * q * LANE:(2 * q + 1) * LANE],
                                          cs_b[:, (2 * q + 1) * LANE:(2 * q + 2) * LANE])
                                for q in range(A_HEADS // 2)], axis=-1)
        xdt = xs * dt_e
        xdt_bf = xdt.astype(BF16)
        xdec_bf = (xdt * jnp.exp(cs_e[CHUNK - 1:CHUNK, :] - cs_e)).astype(BF16)
        grow = jnp.exp(cs_e)

        y_parts = []
        for g in range(A_GROUPS):
            bg = bm[:, g * A_STATE:(g + 1) * A_STATE].astype(BF16)
            cg = cm[:, g * A_STATE:(g + 1) * A_STATE].astype(BF16)
            cb = lax.dot_general(cg, bg, (((1,), (1,)), ((), ())), preferred_element_type=F32)
            st = state_sc[g * gw:(g + 1) * gw, :]
            y_off = lax.dot_general(cg, st.astype(BF16), (((1,), (1,)), ((), ())), preferred_element_type=F32)
            upd = lax.dot_general(xdec_bf[:, g * gw:(g + 1) * gw], bg, (((0,), (0,)), ((), ())),
                                  preferred_element_type=F32)
            for hp in range(A_HEADS // A_GROUPS // 2):
                res = []
                for q in range(2):
                    h = g * (A_HEADS // A_GROUPS) + 2 * hp + q
                    diff = cs_b[:, h * LANE:(h + 1) * LANE] - cs_t[h:h + 1, :]
                    lm = jnp.exp(jnp.where(tril, diff, -jnp.inf))
                    wm = (cb * lm).astype(BF16)
                    res.append(_mm(wm, xdt_bf[:, g * gw + hp * LANE:g * gw + (hp + 1) * LANE]))
                    r_lo = g * gw + (2 * hp + q) * A_HEAD_DIM
                    state_sc[r_lo:r_lo + A_HEAD_DIM, :] = (
                        jnp.exp(cs_t[h:h + 1, CHUNK - 1:CHUNK]) * st[(2 * hp + q) * A_HEAD_DIM:(2 * hp + q + 1) * A_HEAD_DIM, :]
                        + upd[(2 * hp + q) * A_HEAD_DIM:(2 * hp + q + 1) * A_HEAD_DIM, :])
                y_parts.append(jnp.where(lane_lo, res[0], res[1]) + y_off[:, hp * LANE:(hp + 1) * LANE]
                               * grow[:, g * gw + hp * LANE:g * gw + (hp + 1) * LANE])
        y = jnp.concatenate(y_parts, axis=-1) + xs * dsk_ref[...]
        ya = _group_rms(y * _silu(proj_sc[rows, P0_Z:P0_Z + 1024]), gn_ref[...])
        mix_sc[rows, 0:1024] = ya.astype(BF16)

        ug = jax.nn.gelu(proj_sc[rows, P0_U:P0_U + 1024])
        vn = _ln(jax.nn.gelu(proj_sc[rows, P0_V:P0_V + 1024]), lng_ref[...], lnb_ref[...])
        vn_bf = vn.astype(BF16)
        mixed = []
        for g in range(B_GROUPS):
            wt = jnp.where(tril, ws_ref[g], 0.0).astype(BF16)
            mixed.append(_mm(wt, vn_bf[:, g * LANE:(g + 1) * LANE]))
        mixed = jnp.concatenate(mixed, axis=-1) + bsf_ref[...]
        yb = ug * mixed * _silu(proj_sc[rows, P0_G:P0_G + 1024])
        mix_sc[rows, 1024:2048] = yb.astype(BF16)

    x1_ref[...] = x_ref[...] + _mm(mix_sc[...], wout_ref[...])

    @pl.when(c == last)
    def _():
        ssm_ref[...] = state_sc[...]


def _layer0_prompt(x, p):
    nb, seq, _ = x.shape
    T = T0
    const = lambda shape: pl.BlockSpec(shape, lambda b, c: (0,) * len(shape), pipeline_mode=pl.Buffered(1))
    in_specs = [
        pl.BlockSpec((None, T, D_MODEL), lambda b, c: (b, c, 0)),
        const((1, D_MODEL)), const((D_MODEL, P0_ALL)), const((A_CONV, A_CONV_DIM)), const((1, A_CONV_DIM)),
        const((1, LANE)), const((1, LANE)), const((1, 1024)), const((1, 1024)),
        const((1, 1024)), const((1, 1024)), const((B_GROUPS, CHUNK, CHUNK)), const((CHUNK, 1024)),
        const((2048, D_MODEL)), const((CHUNK, CHUNK)), const((LANE, 1024)), const((LANE, 2048)),
    ]
    out_shape = (jax.ShapeDtypeStruct((nb, seq, D_MODEL), F32),
                 jax.ShapeDtypeStruct((nb, A_HEADS * A_HEAD_DIM, A_STATE), F32),
                 jax.ShapeDtypeStruct((nb, SUB, A_CONV_DIM), F32))
    out_specs = (pl.BlockSpec((None, T, D_MODEL), lambda b, c: (b, c, 0)),
                 pl.BlockSpec((None, A_HEADS * A_HEAD_DIM, A_STATE), lambda b, c: (b, 0, 0)),
                 pl.BlockSpec((None, SUB, A_CONV_DIM), lambda b, c: (b, 0, 0)))
    scratch = [pltpu.VMEM((T, P0_MAIN), F32), pltpu.VMEM((SUB + T, A_CONV_DIM), F32),
               pltpu.VMEM((T, A_CONV_DIM), F32), pltpu.VMEM((T, 2048), BF16),
               pltpu.VMEM((A_HEADS * A_HEAD_DIM, A_STATE), F32)]
    return pl.pallas_call(
        _layer0_prompt_kernel, out_shape=out_shape, grid=(nb, seq // T),
        in_specs=in_specs, out_specs=out_specs, scratch_shapes=scratch,
        compiler_params=pltpu.CompilerParams(dimension_semantics=("arbitrary", "arbitrary"),
                                             vmem_limit_bytes=VMEM_LIMIT_BYTES),
        name="layer0_prompt",
    )(x, p["ne"], p["win0"], p["ssd_cw"], p["ssd_cb"], p["dtb"], p["alog"], p["dsk"], p["gn"],
      p["lng"], p["lnb"], p["ws"], p["bsf"], p["wout0"], p["ltri"], p["e64"], p["e128"])


def _lru_gates(xc, wa_ref, ba_ref, wx_ref, bx_ref, lam_ref):
    xb = xc.astype(BF16)
    nblk = wa_ref.shape[0]
    w = wa_ref.shape[1]
    r = jnp.concatenate([_mm(xb[:, j * w:(j + 1) * w], wa_ref[j]) for j in range(nblk)], axis=-1)
    i = jnp.concatenate([_mm(xb[:, j * w:(j + 1) * w], wx_ref[j]) for j in range(nblk)], axis=-1)
    r = _sigmoid(r + ba_ref[...])
    i = _sigmoid(i + bx_ref[...])
    log_a = -LRU_C * r * jax.nn.softplus(-lam_ref[...])
    a = jnp.exp(log_a)
    bt = jnp.sqrt(1.0 - a * a) * (i * xc)
    return a, bt


def _dwconv_rows(pad_ref, w_ref, n_taps, first_row, n_rows, row_step):
    cols = []
    blk = 8 * SUB
    for j in range(pad_ref.shape[1] // LANE):
        lanes = slice(j * LANE, (j + 1) * LANE)
        blocks = []
        for r0 in range(0, n_rows, blk):
            acc = None
            for k in range(n_taps):
                lo = first_row + k * row_step + r0
                term = w_ref[k:k + 1, lanes] * pad_ref[lo:lo + blk, lanes]
                acc = term if acc is None else acc + term
            blocks.append(acc)
        cols.append(jnp.concatenate(blocks, axis=0))
    return jnp.concatenate(cols, axis=-1)


def _layer1_prompt_kernel(x_ref, no_ref, win_ref, ccw_ref, ccb_ref, clg_ref, clb_ref, lcw_ref, lcb_ref,
                          wa_ref, ba_ref, wx_ref, bx_ref, lam_ref, wout_ref, fn_ref,
                          y_ref, ccv_ref, lcv_ref, lru_ref,
                          cpad_sc, lpad_sc, h_sc, mix_sc):
    c = pl.program_id(0)
    last = pl.num_programs(0) - 1
    nb, T, _ = x_ref.shape
    rows_all = nb * T
    chist = CPAD * nb
    lhist = SUB * nb

    @pl.when(c == 0)
    def _():
        cpad_sc[0:chist, :] = jnp.zeros((chist, D_MODEL), F32)
        lpad_sc[0:lhist, :] = jnp.zeros((lhist, D_MODEL), F32)
        h_sc[...] = jnp.zeros_like(h_sc)

    x = jnp.concatenate([x_ref[:, t, :] for t in range(T)], axis=0)
    hn = _rms(x, no_ref[...]).astype(BF16)
    cpad_sc[chist:chist + rows_all, :] = (_mm(hn, win_ref[:, 0:1024])
                                          * _sigmoid(_mm(hn, win_ref[:, 1024:2048])))
    lpad_sc[lhist:lhist + rows_all, :] = _mm(hn, win_ref[:, 3072:4096])
    sgc = _silu(_mm(hn, win_ref[:, 2048:3072]))
    sgd = _silu(_mm(hn, win_ref[:, 4096:5120]))

    cacc = _dwconv_rows(cpad_sc, ccw_ref, C_CONV, (CPAD - (C_CONV - 1)) * nb, rows_all, nb) + ccb_ref[...]
    mix_sc[:, 0:1024] = (_silu(_ln(cacc, clg_ref[...], clb_ref[...])) * sgc).astype(BF16)

    xc = _dwconv_rows(lpad_sc, lcw_ref, D_CONV, (SUB - (D_CONV - 1)) * nb, rows_all, nb) + lcb_ref[...]
    a, bt = _lru_gates(xc, wa_ref, ba_ref, wx_ref, bx_ref, lam_ref)
    h = h_sc[...]
    hs = []
    for t in range(T):
        h = a[t * nb:(t + 1) * nb, :] * h + bt[t * nb:(t + 1) * nb, :]
        hs.append(h)
    h_sc[...] = h
    mix_sc[:, 1024:2048] = (jnp.concatenate(hs, axis=0) * sgd).astype(BF16)

    y = _rms(x + _mm(mix_sc[...], wout_ref[...]), fn_ref[...])
    for t in range(T):
        y_ref[:, t, :] = y[t * nb:(t + 1) * nb, :]

    @pl.when(c == last)
    def _():
        for r in range(CPAD):
            ccv_ref[:, r, :] = cpad_sc[(T + r) * nb:(T + r + 1) * nb, :]
        for r in range(SUB):
            lcv_ref[:, r, :] = lpad_sc[(T + r) * nb:(T + r + 1) * nb, :]
        lru_ref[...] = h

    cpad_sc[0:chist, :] = cpad_sc[rows_all:rows_all + chist, :]
    lpad_sc[0:lhist, :] = lpad_sc[rows_all:rows_all + lhist, :]


def _layer1_prompt(x1, p):
    nb, seq, _ = x1.shape
    T = T1
    rows_all = nb * T
    const = lambda shape: pl.BlockSpec(shape, lambda c: (0,) * len(shape), pipeline_mode=pl.Buffered(1))
    nblk, wblk = p["wa"].shape[0], p["wa"].shape[1]
    in_specs = [
        pl.BlockSpec((nb, T, D_MODEL), lambda c: (0, c, 0)),
        const((1, D_MODEL)), const((D_MODEL, 5120)), const((C_CONV, 1024)), const((1, 1024)), const((1, 1024)),
        const((1, 1024)), const((D_CONV, 1024)), const((1, 1024)),
        const((nblk, wblk, wblk)), const((1, 1024)), const((nblk, wblk, wblk)), const((1, 1024)), const((1, 1024)),
        const((2048, D_MODEL)), const((1, D_MODEL)),
    ]
    out_shape = (jax.ShapeDtypeStruct((nb, seq, D_MODEL), F32),
                 jax.ShapeDtypeStruct((nb, CPAD, 1024), F32),
                 jax.ShapeDtypeStruct((nb, SUB, 1024), F32),
                 jax.ShapeDtypeStruct((nb, 1024), F32))
    out_specs = (pl.BlockSpec((nb, T, D_MODEL), lambda c: (0, c, 0)),
                 pl.BlockSpec((nb, CPAD, 1024), lambda c: (0, 0, 0)),
                 pl.BlockSpec((nb, SUB, 1024), lambda c: (0, 0, 0)),
                 pl.BlockSpec((nb, 1024), lambda c: (0, 0)))
    scratch = [pltpu.VMEM(((CPAD + T) * nb, 1024), F32), pltpu.VMEM(((SUB + T) * nb, 1024), F32),
               pltpu.VMEM((nb, 1024), F32), pltpu.VMEM((rows_all, 2048), BF16)]
    return pl.pallas_call(
        _layer1_prompt_kernel, out_shape=out_shape, grid=(seq // T,),
        in_specs=in_specs, out_specs=out_specs, scratch_shapes=scratch,
        compiler_params=pltpu.CompilerParams(dimension_semantics=("arbitrary",),
                                             vmem_limit_bytes=VMEM_LIMIT_BYTES),
        name="layer1_prompt",
    )(x1, p["no"], p["win1"], p["ccw"], p["ccb"], p["clg"], p["clb"], p["lcw"], p["lcb"],
      p["wa"], p["ba"], p["wx"], p["bx"], p["lam"], p["wout1"], p["fn"])


def _sample0_proj_kernel(x_ref, ne_ref, win_ref, sbuf_ref, cw_ref, cb_ref, dtb_ref, alog_ref, dsk_ref,
                         lng_ref, lnb_ref, w00_ref, b00_ref, e64_ref,
                         nbuf_ref, xdtt_ref, dec_ref, bm_ref, ct_ref, xsd_ref, sz_ref, ybg_ref, vn_ref):
    hn = _rms(x_ref[...], ne_ref[...]).astype(BF16)
    xbc = _mm(hn, win_ref[:, P0_MAIN:P0_ALL])
    acc = cb_ref[...] + cw_ref[A_CONV - 1:A_CONV, :] * xbc
    for k in range(A_CONV - 1):
        acc = acc + cw_ref[k:k + 1, :] * sbuf_ref[k]
    for k in range(A_CONV - 2):
        nbuf_ref[k] = sbuf_ref[k + 1]
    nbuf_ref[A_CONV - 2] = xbc
    xcv = _silu(acc)
    xs = xcv[:, 0:1024]
    nbc = A_GROUPS * A_STATE
    bm_ref[...] = xcv[:, 1024:1024 + nbc]
    ct_ref[...] = xcv[:, 1024 + nbc:A_CONV_DIM].T
    dt = jax.nn.softplus(_mm(hn, win_ref[:, P0_DT:P0_DT + LANE]) + dtb_ref[...])
    dec = jnp.exp(dt * (-jnp.exp(alog_ref[...])))
    xdtt_ref[...] = (xs * _mm_exact_lhs(dt, e64_ref[...])).T.astype(BF16)
    dec_ref[...] = dec
    xsd_ref[...] = xs * dsk_ref[...]
    sz_ref[...] = _silu(_mm(hn, win_ref[:, P0_Z:P0_Z + 1024]))
    vn = _ln(jax.nn.gelu(_mm(hn, win_ref[:, P0_V:P0_V + 1024])), lng_ref[...], lnb_ref[...])
    vn_ref[...] = vn
    mixed = w00_ref[...] * vn + b00_ref[...]
    ybg_ref[...] = (jax.nn.gelu(_mm(hn, win_ref[:, P0_U:P0_U + 1024])) * mixed
                    * _silu(_mm(hn, win_ref[:, P0_G:P0_G + 1024])))


def _sample0_proj(xs, sbuf, p):
    n = xs.shape[0]
    full = lambda a: pl.BlockSpec(a.shape, lambda i: (0,) * a.ndim)
    args = (xs, p["ne"], p["win0"], sbuf, p["ssd_cw"], p["ssd_cb"], p["dtb"], p["alog"], p["dsk"],
            p["lng"], p["lnb"], p["w00"], p["b00"], p["e64"])
    nbc = A_GROUPS * A_STATE
    shapes = [sbuf.shape, (1024, n), (n, LANE), (n, nbc), (nbc, n), (n, 1024), (n, 1024),
              (n, 1024), (n, 1024)]
    dtypes = [F32, BF16] + [F32] * 7
    return pl.pallas_call(
        _sample0_proj_kernel, grid=(1,),
        out_shape=tuple(jax.ShapeDtypeStruct(s, d) for s, d in zip(shapes, dtypes)),
        in_specs=[full(a) for a in args],
        out_specs=tuple(pl.BlockSpec(s, lambda i, nd=len(s): (0,) * nd) for s in shapes),
        compiler_params=pltpu.CompilerParams(dimension_semantics=("arbitrary",), vmem_limit_bytes=VMEM_LIMIT_BYTES),
        name="sample0_proj",
    )(*args)


def _sample0_state_kernel(dec_ref, h_ref, xt_ref, bm_ref, ct_ref, hn_ref, yt_ref):
    i = pl.program_id(0)
    tb = h_ref.shape[0]
    ntok = bm_ref.shape[0]
    hpg = A_HEADS // A_GROUPS
    gw = hpg * A_HEAD_DIM

    @pl.when(i == 0)
    def _():
        yt_ref[...] = jnp.zeros_like(yt_ref)

    row_id = lax.broadcasted_iota(jnp.int32, (ntok, A_STATE), 0)
    lane_id = lax.broadcasted_iota(jnp.int32, (A_STATE, ntok), 1)
    acc = [jnp.zeros((gw, ntok), F32) for _ in range(A_GROUPS)]
    for b in range(tb):
        tok = i * tb + b
        for g in range(A_GROUPS):
            bmask = jnp.where(row_id == tok, bm_ref[:, g * A_STATE:(g + 1) * A_STATE], 0.0).astype(BF16)
            upd = _mm(xt_ref[g * gw:(g + 1) * gw, :], bmask)
            parts = []
            for h in range(hpg):
                lo = g * gw + h * A_HEAD_DIM
                hn = dec_ref[tok, g * hpg + h] * h_ref[b, lo:lo + A_HEAD_DIM, :] + upd[h * A_HEAD_DIM:(h + 1) * A_HEAD_DIM, :]
                hn_ref[b, lo:lo + A_HEAD_DIM, :] = hn
                parts.append(hn)
            cmask = jnp.where(lane_id == tok, ct_ref[g * A_STATE:(g + 1) * A_STATE, :], 0.0).astype(BF16)
            acc[g] = acc[g] + _mm(jnp.concatenate(parts, axis=0).astype(BF16), cmask)
    for g in range(A_GROUPS):
        yt_ref[g * gw:(g + 1) * gw, :] += acc[g]


def _sample0_state(dec, h0, xdtt, bm, ct):
    n = h0.shape[0]
    tb = SUB
    rows = A_HEADS * A_HEAD_DIM
    blk = pl.BlockSpec((tb, rows, A_STATE), lambda i: (i, 0, 0))
    full = lambda a: pl.BlockSpec(a.shape, lambda i: (0,) * a.ndim)
    return pl.pallas_call(
        _sample0_state_kernel, grid=(n // tb,),
        out_shape=(jax.ShapeDtypeStruct(h0.shape, F32), jax.ShapeDtypeStruct((rows, n), F32)),
        in_specs=[pl.BlockSpec(memory_space=pltpu.SMEM), blk, full(xdtt), full(bm), full(ct)],
        out_specs=(blk, pl.BlockSpec((rows, n), lambda i: (0, 0))),
        compiler_params=pltpu.CompilerParams(dimension_semantics=("arbitrary",), vmem_limit_bytes=VMEM_LIMIT_BYTES),
        name="sample0_state",
    )(dec, h0, xdtt, bm, ct)


def _sample_mid_kernel(yt_ref, xsd_ref, sz_ref, ybg_ref, x_ref, gn_ref, wout0_ref,
                       no_ref, win_ref, lbuf_ref, h0_ref, lcw_ref, lcb_ref,
                       wa_ref, ba_ref, wx_ref, bx_ref, lam_ref,
                       x1_ref, glu_ref, sgc_ref, ydg_ref, nlbuf_ref, hnew_ref):
    ya = _group_rms((yt_ref[...].T + xsd_ref[...]) * sz_ref[...], gn_ref[...])
    mix = jnp.concatenate([ya, ybg_ref[...]], axis=-1).astype(BF16)
    x1 = x_ref[...] + _mm(mix, wout0_ref[...])
    x1_ref[...] = x1
    hn = _rms(x1, no_ref[...]).astype(BF16)
    glu_ref[...] = _mm(hn, win_ref[:, 0:1024]) * _sigmoid(_mm(hn, win_ref[:, 1024:2048]))
    sgc_ref[...] = _silu(_mm(hn, win_ref[:, 2048:3072]))
    xd = _mm(hn, win_ref[:, 3072:4096])
    acc = lcb_ref[...] + lcw_ref[D_CONV - 1:D_CONV, :] * xd
    for k in range(D_CONV - 1):
        acc = acc + lcw_ref[k:k + 1, :] * lbuf_ref[k]
    for k in range(D_CONV - 2):
        nlbuf_ref[k] = lbuf_ref[k + 1]
    nlbuf_ref[D_CONV - 2] = xd
    a, bt = _lru_gates(acc, wa_ref, ba_ref, wx_ref, bx_ref, lam_ref)
    h = a * h0_ref[...] + bt
    hnew_ref[...] = h
    ydg_ref[...] = h * _silu(_mm(hn, win_ref[:, 4096:5120]))


def _sample_mid(yt, xsd, sz, ybg, xs, lbuf, h0, p):
    n = xs.shape[0]
    full = lambda a: pl.BlockSpec(a.shape, lambda i: (0,) * a.ndim)
    args = (yt, xsd, sz, ybg, xs, p["gn"], p["wout0"], p["no"], p["win1"], lbuf, h0, p["lcw"], p["lcb"],
            p["wa"], p["ba"], p["wx"], p["bx"], p["lam"])
    shapes = [(n, 1024), (n, 1024), (n, 1024), (n, 1024), lbuf.shape, (n, 1024)]
    return pl.pallas_call(
        _sample_mid_kernel, grid=(1,),
        out_shape=tuple(jax.ShapeDtypeStruct(s, F32) for s in shapes),
        in_specs=[full(a) for a in args],
        out_specs=tuple(pl.BlockSpec(s, lambda i, nd=len(s): (0,) * nd) for s in shapes),
        compiler_params=pltpu.CompilerParams(dimension_semantics=("arbitrary",), vmem_limit_bytes=VMEM_LIMIT_BYTES),
        name="sample_mid",
    )(*args)


def _sample1_tail_kernel(cbuf_ref, glu_ref, sgc_ref, ydg_ref, x1_ref, ccw_ref, ccb_ref, clg_ref, clb_ref,
                         wout_ref, fn_ref, ncbuf_ref, y_ref):
    hist = C_CONV - 1
    glu = glu_ref[...]
    acc = ccb_ref[...] + ccw_ref[hist:C_CONV, :] * glu
    for k in range(hist):
        acc = acc + ccw_ref[k:k + 1, :] * cbuf_ref[k]
    for k in range(hist - 1):
        ncbuf_ref[k] = cbuf_ref[k + 1]
    ncbuf_ref[hist - 1] = glu
    yc =_silu(_ln(acc, clg_ref[...], clb_ref[...])) * sgc_ref[...]
    mix = jnp.concatenate([yc, ydg_ref[...]], axis=-1).astype(BF16)
    out = x1_ref[...] + _mm(mix, wout_ref[...])
    y_ref[...] = _rms(out, fn_ref[...])


def _sample1_tail(cbuf, glu, sgc, ydg, x1, p):
    n = x1.shape[0]
    tb = 32
    row = lambda w: pl.BlockSpec((tb, w), lambda i: (i, 0))
    blk = pl.BlockSpec((cbuf.shape[0], tb, cbuf.shape[2]), lambda i: (0, i, 0))
    const = lambda a: pl.BlockSpec(a.shape, lambda i: (0,) * a.ndim, pipeline_mode=pl.Buffered(1))
    consts = (p["ccw"], p["ccb"], p["clg"], p["clb"], p["wout1"], p["fn"])
    return pl.pallas_call(
        _sample1_tail_kernel, grid=(n // tb,),
        out_shape=(jax.ShapeDtypeStruct(cbuf.shape, F32), jax.ShapeDtypeStruct((n, 1024), F32)),
        in_specs=[blk, row(1024), row(1024), row(1024), row(1024)] + [const(a) for a in consts],
        out_specs=(blk, row(1024)),
        compiler_params=pltpu.CompilerParams(dimension_semantics=("arbitrary",), vmem_limit_bytes=VMEM_LIMIT_BYTES),
        name="sample1_tail",
    )(cbuf, glu, sgc, ydg, x1, *consts)


def _block_diag(w, per_block):
    nh, d, _ = w.shape
    nblk = nh // per_block
    w = w.reshape(nblk, per_block, d, d)
    eye = jnp.eye(per_block, dtype=w.dtype)
    bd = jnp.einsum("nhij,hk->nhikj", w, eye)
    return bd.reshape(nblk, per_block * d, per_block * d).astype(BF16)


def _repack_w_in_even_kernel(w_ref, o_ref):
    n_xbc_end = 1024 + A_CONV_DIM
    cb = w_ref.shape[1]
    eye = (lax.broadcasted_iota(jnp.int32, (cb, cb), 0) == lax.broadcasted_iota(jnp.int32, (cb, cb), 1)).astype(BF16)
    tr = lambda piece: _mm_nt(eye, piece.astype(BF16)).astype(BF16)
    o_ref[:, P0_Z:P0_U] = tr(w_ref[0:1024, :])
    o_ref[:, P0_U:P0_DT] = tr(w_ref[n_xbc_end + A_HEADS:, :])
    o_ref[:, P0_DT:P0_XBC] = tr(jnp.concatenate([w_ref[n_xbc_end:n_xbc_end + A_HEADS, :],
                                                 jnp.zeros((LANE - A_HEADS, cb), F32)], axis=0))
    o_ref[:, P0_XBC:P0_ALL] = tr(w_ref[1024:n_xbc_end, :])


def _repack_w_in_even(w_t):
    rows, cols = w_t.shape
    cb = 2 * LANE
    return pl.pallas_call(
        _repack_w_in_even_kernel, out_shape=jax.ShapeDtypeStruct((cols, P0_ALL), BF16), grid=(cols // cb,),
        in_specs=[pl.BlockSpec((rows, cb), lambda i: (0, i))],
        out_specs=pl.BlockSpec((cb, P0_ALL), lambda i: (i, 0)),
        compiler_params=pltpu.CompilerParams(dimension_semantics=("arbitrary",), vmem_limit_bytes=VMEM_LIMIT_BYTES),
        name="repack_w_in_even",
    )(w_t)


def _prepare(norm_even, w_in_even, ssd_conv_w, ssd_conv_b, ssd_dt_bias, ssd_a_log, ssd_d, ssd_norm,
             gmlp_ln_g, gmlp_ln_b, gmlp_w_s, gmlp_b_s, w_out_even,
             norm_odd, w_in_odd, ccv_w, ccv_b, ccv_ln_g, ccv_ln_b,
             lru_conv_w, lru_conv_b, lru_wa, lru_ba, lru_wx, lru_bx, lru_lambda, w_out_odd, final_norm):
    row = lambda v: v.reshape(1, -1).astype(F32)
    heads = np.arange(LANE)[:, None]
    p = {
        "ne": row(norm_even[0]),
        "win0": _repack_w_in_even(jnp.transpose(w_in_even[0])),
        "ssd_cw": ssd_conv_w[0], "ssd_cb": row(ssd_conv_b[0]),
        "dtb": jnp.pad(row(ssd_dt_bias[0]), ((0, 0), (0, LANE - A_HEADS))),
        "alog": jnp.pad(row(ssd_a_log[0]), ((0, 0), (0, LANE - A_HEADS))),
        "dsk": row(jnp.repeat(ssd_d[0], A_HEAD_DIM)),
        "gn": row(ssd_norm[0]),
        "lng": row(gmlp_ln_g[0]), "lnb": row(gmlp_ln_b[0]),
        "ws": gmlp_w_s[0],
        "bsf": jnp.repeat(gmlp_b_s[0].T, LANE, axis=1),
        "w00": row(jnp.repeat(gmlp_w_s[0, :, 0, 0], LANE)), "b00": row(jnp.repeat(gmlp_b_s[0, :, 0], LANE)),
        "wout0": w_out_even[0].astype(BF16),
        "ltri": jnp.asarray(np.arange(CHUNK)[:, None] >= np.arange(CHUNK)[None, :], BF16),
        "e64": jnp.asarray(np.arange(1024)[None, :] // A_HEAD_DIM == heads, BF16),
        "e128": jnp.asarray(np.arange(2048)[None, :] // LANE == heads, BF16),
        "no": row(norm_odd[0]),
        "win1": w_in_odd[0].astype(BF16),
        "ccw": ccv_w[0], "ccb": row(ccv_b[0]), "clg": row(ccv_ln_g[0]), "clb": row(ccv_ln_b[0]),
        "lcw": lru_conv_w[0], "lcb": row(lru_conv_b[0]),
        "wa": _block_diag(lru_wa[0], 4), "ba": row(lru_ba[0]),
        "wx": _block_diag(lru_wx[0], 4), "bx": row(lru_bx[0]),
        "lam": row(lru_lambda[0]),
        "wout1": w_out_odd[0].astype(BF16),
        "fn": row(final_norm),
    }
    return p


def kernel(x_prompt, x_sample, state_ssm, state_ssd_conv, state_ccv, state_lru_conv, state_lru, norm_even, w_in_even, ssd_conv_w, ssd_conv_b, ssd_dt_bias, ssd_a_log, ssd_d, ssd_norm, gmlp_ln_g, gmlp_ln_b, gmlp_w_s, gmlp_b_s, w_out_even, norm_odd, w_in_odd, ccv_w, ccv_b, ccv_ln_g, ccv_ln_b, lru_conv_w, lru_conv_b, lru_wa, lru_ba, lru_wx, lru_bx, lru_lambda, w_out_odd, final_norm):
    p = _prepare(norm_even, w_in_even, ssd_conv_w, ssd_conv_b, ssd_dt_bias, ssd_a_log, ssd_d, ssd_norm,
                 gmlp_ln_g, gmlp_ln_b, gmlp_w_s, gmlp_b_s, w_out_even,
                 norm_odd, w_in_odd, ccv_w, ccv_b, ccv_ln_g, ccv_ln_b,
                 lru_conv_w, lru_conv_b, lru_wa, lru_ba, lru_wx, lru_bx, lru_lambda, w_out_odd, final_norm)
    nb = x_prompt.shape[0]
    ns = x_sample.shape[0]

    x1, ssm_p, sbuf_p = _layer0_prompt(x_prompt, p)
    y_prompt, cbuf_p, lbuf_p, lru_p = _layer1_prompt(x1, p)

    xs = x_sample.reshape(ns, D_MODEL)
    tap_major = lambda s: jnp.transpose(s[0], (1, 0, 2))
    seq_major = lambda s: jnp.transpose(s, (1, 0, 2))[None]
    nsbuf, xdtt, dec, bm, ct, xsd, sz, ybg, vn = _sample0_proj(xs, tap_major(state_ssd_conv), p)
    ssm_s, y_ssd = _sample0_state(dec[:, :A_HEADS], state_ssm[0].reshape(ns, A_HEADS * A_HEAD_DIM, A_STATE),
                                  xdtt, bm, ct)
    x1s, glu, sgc, ydg, nlbuf, lru_s = _sample_mid(y_ssd, xsd, sz, ybg, xs, tap_major(state_lru_conv),
                                                   state_lru[0], p)
    ncbuf, y_sample = _sample1_tail(tap_major(state_ccv), glu, sgc, ydg, x1s, p)

    return (y_prompt,
            y_sample.reshape(ns, 1, D_MODEL),
            ssm_p.reshape(1, nb, A_HEADS, A_HEAD_DIM, A_STATE),
            ssm_s.reshape(1, ns, A_HEADS, A_HEAD_DIM, A_STATE),
            sbuf_p[:, SUB - (A_CONV - 1):, :][None],
            seq_major(nsbuf),
            vn.reshape(1, ns, 1, 1024),
            cbuf_p[:, CPAD - (C_CONV - 1):, :][None],
            seq_major(ncbuf),
            lbuf_p[:, SUB - (D_CONV - 1):, :][None],
            seq_major(nlbuf),
            lru_p[None],
            lru_s[None])
```
